```python
import math
import jax, jax.numpy as jnp
from jax import lax
import numpy as np

D_MODEL = 1024
BATCH = 16
SEQ = 4096
DEPTH = 2

N_META = 16
BLOCK = 128
PAD_LEN = BLOCK - N_META
EPS = 1e-6
NEG_INF = -1e30

N_BRANCH = 4
D_BRANCH = 256

FOX_HEADS = 4
FOX_DH = 64

MLA_HEADS = 4
MLA_NOPE = 64
MLA_ROPE = 32
MLA_DV = 64
MLA_Q_RANK = 192
MLA_KV_RANK = 128
ROPE_BASE = 10000.0

GDN_HEADS = 4
GDN_DK = 64
GDN_DV = 64
GDN_CONV = 4
GDN_CHUNK = 64

LRU_WIDTH = 256
LRU_BLOCKS = 4
LRU_CONV = 4
LRU_C = 8.0

D_FF = 2816

OFF_FOX_QKV = 0
OFF_FOX_F = OFF_FOX_QKV + 3 * FOX_HEADS * FOX_DH
OFF_MLA_CQ = OFF_FOX_F + FOX_HEADS
OFF_MLA_CKV = OFF_MLA_CQ + MLA_Q_RANK
OFF_MLA_KR = OFF_MLA_CKV + MLA_KV_RANK
OFF_GDN_QKV = OFF_MLA_KR + MLA_ROPE
OFF_GDN_A = OFF_GDN_QKV + GDN_HEADS * (2 * GDN_DK + GDN_DV)
OFF_GDN_B = OFF_GDN_A + GDN_HEADS
OFF_GDN_G = OFF_GDN_B + GDN_HEADS
OFF_LRU = OFF_GDN_G + GDN_HEADS * GDN_DV
N_IN = OFF_LRU + LRU_WIDTH

kernel_name = "hybrid_fox_mla_gdn_rglru_macaron"


def rmsnorm(x, g):
    xf = x.astype(jnp.float32)
    y = xf * lax.rsqrt(jnp.mean(xf * xf, axis=-1, keepdims=True) + EPS)
    return y.astype(x.dtype) * g


def l2norm(x):
    return x * lax.rsqrt(jnp.sum(x * x, axis=-1, keepdims=True) + EPS)


def swiglu(h, wi, wo):
    gu = h @ wi
    g, u = jnp.split(gu, 2, axis=-1)
    return (jax.nn.silu(g) * u) @ wo


def causal_dwconv(x, w):
    K, C = w.shape
    return lax.conv_general_dilated(
        x, w[:, None, :].astype(x.dtype), window_strides=(1,), padding=[(K - 1, 0)],
        dimension_numbers=('NWC', 'WIO', 'NWC'), feature_group_count=C)


def rope(x, cos, sin):
    half = x.shape[-1] // 2
    x1, x2 = x[..., :half], x[..., half:]
    return jnp.concatenate([x1 * cos - x2 * sin, x2 * cos + x1 * sin], axis=-1)


def blocked_causal_attention(q, k, v, scale, cum=None):
    B, H, T, dk = q.shape
    nb = T // BLOCK
    kpos = jnp.arange(T)
    key_ok = kpos >= PAD_LEN
    q_blocks = jnp.moveaxis(q.reshape(B, H, nb, BLOCK, dk), 2, 0)
    xs = (jnp.arange(nb), q_blocks)
    if cum is not None:
        xs = xs + (jnp.moveaxis(cum.reshape(B, H, nb, BLOCK), 2, 0),)

    def one_block(blk):
        i, q_i = blk[0], blk[1]
        s = jnp.einsum('bhqd,bhkd->bhqk', q_i, k, preferred_element_type=jnp.float32) * scale
        if cum is not None:
            s = s + blk[2][..., :, None] - cum[:, :, None, :]
        qpos = i * BLOCK + jnp.arange(BLOCK)
        mask = (kpos[None, :] <= qpos[:, None]) & key_ok[None, :]
        s = jnp.where(mask, s, NEG_INF)
        prob = jax.nn.softmax(s, axis=-1)
        return jnp.einsum('bhqk,bhkd->bhqd', prob.astype(v.dtype), v)

    out = lax.map(one_block, xs)
    return jnp.moveaxis(out, 0, 2).reshape(B, H, T, v.shape[-1])


def fox_branch(p, b_f):
    B, T, _ = p.shape
    qkv = p[..., OFF_FOX_QKV:OFF_FOX_F].reshape(B, T, 3, FOX_HEADS, FOX_DH)
    q = qkv[:, :, 0].transpose(0, 2, 1, 3)
    k = qkv[:, :, 1].transpose(0, 2, 1, 3)
    v = qkv[:, :, 2].transpose(0, 2, 1, 3)
    log_f = jax.nn.log_sigmoid((p[..., OFF_FOX_F:OFF_MLA_CQ] + b_f).astype(jnp.float32))
    cum = jnp.cumsum(log_f, axis=1).transpose(0, 2, 1)
    o = blocked_causal_attention(q, k, v, FOX_DH ** -0.5, cum)
    return o.transpose(0, 2, 1, 3).reshape(B, T, FOX_HEADS * FOX_DH)


def mla_branch(p, g_qn, w_q_up, g_kvn, w_kv_up, cos, sin):
    B, T, _ = p.shape
    cq = rmsnorm(p[..., OFF_MLA_CQ:OFF_MLA_CKV], g_qn)
    q = (cq @ w_q_up).reshape(B, T, MLA_HEADS, MLA_NOPE + MLA_ROPE)
    ckv = rmsnorm(p[..., OFF_MLA_CKV:OFF_MLA_KR], g_kvn)
    kv = (ckv @ w_kv_up).reshape(B, T, MLA_HEADS, MLA_NOPE + MLA_DV)
    k_rope = rope(p[..., OFF_MLA_KR:OFF_GDN_QKV], cos, sin)
    q_rope = rope(q[..., MLA_NOPE:], cos[:, None], sin[:, None])
    q = jnp.concatenate([q[..., :MLA_NOPE], q_rope], axis=-1)
    k = jnp.concatenate([kv[..., :MLA_NOPE],
                         jnp.broadcast_to(k_rope[:, :, None], (B, T, MLA_HEADS, MLA_ROPE))], axis=-1)
    v = kv[..., MLA_NOPE:]
    o = blocked_causal_attention(q.transpose(0, 2, 1, 3), k.transpose(0, 2, 1, 3),
                                 v.transpose(0, 2, 1, 3), (MLA_NOPE + MLA_ROPE) ** -0.5)
    return o.transpose(0, 2, 1, 3).reshape(B, T, MLA_HEADS * MLA_DV)


def gdn_branch(p, conv_w, a_log, dt_bias, g_on):
    B, T, _ = p.shape
    H, DK, DV, C = GDN_HEADS, GDN_DK, GDN_DV, GDN_CHUNK
    f32 = jnp.float32
    qkv = jax.nn.silu(causal_dwconv(p[..., OFF_GDN_QKV:OFF_GDN_A], conv_w)).astype(f32)
    q = l2norm(qkv[..., :H * DK].reshape(B, T, H, DK)) * DK ** -0.5
    k = l2norm(qkv[..., H * DK:2 * H * DK].reshape(B, T, H, DK))
    v = qkv[..., 2 * H * DK:].reshape(B, T, H, DV)
    beta = jax.nn.sigmoid(p[..., OFF_GDN_B:OFF_GDN_G].astype(f32))
    g = -jnp.exp(a_log.astype(f32)) * jax.nn.softplus(
        p[..., OFF_GDN_A:OFF_GDN_B].astype(f32) + dt_bias.astype(f32))
    nc = T // C

    def chunks(t):
        return jnp.moveaxis(t, 2, 1).reshape((B, H, nc, C) + t.shape[3:])

    q, k, v, beta, g = chunks(q), chunks(k), chunks(v), chunks(beta), chunks(g)
    G = jnp.cumsum(g, axis=-1)
    idx = jnp.arange(C)
    strict = idx[:, None] > idx[None, :]
    incl = idx[:, None] >= idx[None, :]
    decay = jnp.exp(jnp.where(incl, G[..., :, None] - G[..., None, :], NEG_INF))
    kb = k * beta[..., None]
    vb = v * beta[..., None]
    m = jnp.eye(C, dtype=f32) + jnp.where(
        strict, jnp.einsum('bhnik,bhnjk->bhnij', kb, k) * decay, 0.0)
    rhs = jnp.concatenate([kb * jnp.exp(G)[..., None], vb], axis=-1)
    sol = lax.linalg.triangular_solve(m, rhs, left_side=True, lower=True, unit_diagonal=True)
    w, u = sol[..., :DK], sol[..., DK:]
    qk = jnp.where(incl, jnp.einsum('bhnik,bhnjk->bhnij', q, k) * decay, 0.0)
    q_dec = q * jnp.exp(G)[..., None]
    k_dec = k * jnp.exp(G[..., -1:] - G)[..., None]
    g_last = jnp.exp(G[..., -1])
    xs = (jnp.moveaxis(q_dec, 2, 0), jnp.moveaxis(k_dec, 2, 0), jnp.moveaxis(w, 2, 0),
          jnp.moveaxis(u, 2, 0), jnp.moveaxis(qk, 2, 0), jnp.moveaxis(g_last, 2, 0))

    def step(S, inp):
        q_c, k_c, w_c, u_c, qk_c, gl_c = inp
        v_new = u_c - jnp.einsum('bhck,bhkv->bhcv', w_c, S)
        o_c = jnp.einsum('bhck,bhkv->bhcv', q_c, S) + jnp.einsum('bhij,bhjv->bhiv', qk_c, v_new)
        S = S * gl_c[..., None, None] + jnp.einsum('bhck,bhcv->bhkv', k_c, v_new)
        return S, o_c

    S0 = jnp.zeros((B, H, DK, DV), f32)
    _, o = lax.scan(step, S0, xs)
    o = jnp.moveaxis(o, 0, 2).reshape(B, H, T, DV).transpose(0, 2, 1, 3)
    gate = jax.nn.silu(p[..., OFF_GDN_G:OFF_LRU].astype(f32)).reshape(B, T, H, DV)
    o = rmsnorm(o, g_on) * gate
    return o.reshape(B, T, H * DV).astype(p.dtype)


def rglru_branch(p, valid, conv_w, conv_b, w_a, b_a, w_x, b_x, lam):
    B, T, _ = p.shape
    f32 = jnp.float32
    xr = causal_dwconv(p[..., OFF_LRU:N_IN], conv_w) + conv_b
    xr = jnp.where(valid[None, :, None], xr, 0)
    xb = xr.reshape(B, T, LRU_BLOCKS, LRU_WIDTH // LRU_BLOCKS)
    r = jax.nn.sigmoid(jnp.einsum('btni,nij->btnj', xb, w_a).reshape(B, T, LRU_WIDTH) + b_a).astype(f32)
    ig = jax.nn.sigmoid(jnp.einsum('btni,nij->btnj', xb, w_x).reshape(B, T, LRU_WIDTH) + b_x).astype(f32)
    log_a = -LRU_C * r * jax.nn.softplus(-lam.astype(f32))
    a = jnp.exp(log_a)
    b = jnp.sqrt(-jnp.expm1(2.0 * log_a)) * ig * xr.astype(f32)

    def combine(e1, e2):
        return (e1[0] * e2[0], e2[0] * e1[1] + e2[1])

    _, h = lax.associative_scan(combine, (a, b), axis=1)
    return h.astype(p.dtype)


def hybrid_mixer(u, valid, cos, sin, w_in, fox_bf, mla_gq, mla_wq, mla_gkv, mla_wkv,
                 gdn_conv, gdn_alog, gdn_dtb, gdn_gon, lru_conv, lru_conv_b, lru_wa, lru_ba,
                 lru_wx, lru_bx, lru_lam, w_gate, b_gate, w_branch, w_out):
    p = u @ w_in
    ys = (fox_branch(p, fox_bf),
          mla_branch(p, mla_gq, mla_wq, mla_gkv, mla_wkv, cos, sin),
          gdn_branch(p, gdn_conv, gdn_alog, gdn_dtb, gdn_gon),
          rglru_branch(p, valid, lru_conv, lru_conv_b, lru_wa, lru_ba, lru_wx, lru_bx, lru_lam))
    merged = jax.nn.sigmoid(u @ w_gate[0] + b_gate[0]) * (ys[0] @ w_branch[0])
    for n in range(1, N_BRANCH):
        merged = merged + jax.nn.sigmoid(u @ w_gate[n] + b_gate[n]) * (ys[n] @ w_branch[n])
    return merged @ w_out


def setup_inputs(seed: int = 0) -> dict:
    key = jax.random.key(seed)
    k = jax.random.split(key, 32)
    f32 = jnp.float32
    D, L, F = D_MODEL, DEPTH, D_FF

    def nrm(i, shape, scale):
        return scale * jax.random.normal(k[i], shape, f32)

    def gain(i, shape):
        return 1.0 + 0.02 * jax.random.normal(k[i], shape, f32)

    u_a = jax.random.uniform(k[22], (L, LRU_WIDTH), f32, 0.9, 0.999)
    a_base = u_a ** (1.0 / LRU_C)
    lru_lam = jnp.log(a_base) - jnp.log1p(-a_base)
    dt = jnp.exp(jax.random.uniform(k[14], (L, GDN_HEADS), f32, math.log(1e-3), math.log(1e-1)))
    gdn_dtb = dt + jnp.log(-jnp.expm1(-dt))
    gdn_alog = jnp.log(jax.random.uniform(k[13], (L, GDN_HEADS), f32, 1.0, 16.0))
    return {
        "x": nrm(0, (BATCH, SEQ, D), 1.0),
        "meta": nrm(1, (N_META, D), 1.0),
        "ln_ffn1": gain(2, (L, D)),
        "ffn1_wi": nrm(3, (L, D, 2 * F), D ** -0.5),
        "ffn1_wo": nrm(4, (L, F, D), F ** -0.5),
        "ln_mix": gain(5, (L, D)),
        "w_in": nrm(6, (L, D, N_IN), D ** -0.5),
        "fox_bf": 3.0 + nrm(7, (L, FOX_HEADS), 0.1),
        "mla_gq": gain(8, (L, MLA_Q_RANK)),
        "mla_wq": nrm(9, (L, MLA_Q_RANK, MLA_HEADS * (MLA_NOPE + MLA_ROPE)), MLA_Q_RANK ** -0.5),
        "mla_gkv": gain(10, (L, MLA_KV_RANK)),
        "mla_wkv": nrm(11, (L, MLA_KV_RANK, MLA_HEADS * (MLA_NOPE + MLA_DV)), MLA_KV_RANK ** -0.5),
        "gdn_conv": nrm(12, (L, GDN_CONV, GDN_HEADS * (2 * GDN_DK + GDN_DV)), GDN_CONV ** -0.5),
        "gdn_alog": gdn_alog,
        "gdn_dtb": gdn_dtb,
        "gdn_gon": gain(15, (L, GDN_DV)),
        "lru_conv": nrm(16, (L, LRU_CONV, LRU_WIDTH), LRU_CONV ** -0.5),
        "lru_conv_b": nrm(17, (L, LRU_WIDTH), 0.01),
        "lru_wa": nrm(18, (L, LRU_BLOCKS, LRU_WIDTH // LRU_BLOCKS, LRU_WIDTH // LRU_BLOCKS), (LRU_WIDTH // LRU_BLOCKS) ** -0.5),
        "lru_ba": nrm(19, (L, LRU_WIDTH), 0.01),
        "lru_wx": nrm(20, (L, LRU_BLOCKS, LRU_WIDTH // LRU_BLOCKS, LRU_WIDTH // LRU_BLOCKS), (LRU_WIDTH // LRU_BLOCKS) ** -0.5),
        "lru_bx": nrm(21, (L, LRU_WIDTH), 0.01),
        "lru_lam": lru_lam,
        "w_gate": nrm(23, (L, N_BRANCH, D, D), D ** -0.5),
        "b_gate": nrm(24, (L, N_BRANCH, D), 0.01),
        "w_branch": nrm(25, (L, N_BRANCH, D_BRANCH, D), D_BRANCH ** -0.5),
        "w_out": nrm(26, (L, D, D), D ** -0.5),
        "ln_ffn2": gain(27, (L, D)),
        "ffn2_wi": nrm(28, (L, D, 2 * F), D ** -0.5),
        "ffn2_wo": nrm(29, (L, F, D), F ** -0.5),
        "ln_final": gain(30, (D,)),
    }


def reference(x, meta, ln_ffn1, ffn1_wi, ffn1_wo, ln_mix, w_in, fox_bf, mla_gq, mla_wq,
              mla_gkv, mla_wkv, gdn_conv, gdn_alog, gdn_dtb, gdn_gon, lru_conv, lru_conv_b,
              lru_wa, lru_ba, lru_wx, lru_bx, lru_lam, w_gate, b_gate, w_branch, w_out,
              ln_ffn2, ffn2_wi, ffn2_wo, ln_final):
    B, S, D = x.shape
    T = BLOCK + S
    h = jnp.concatenate([jnp.zeros((B, PAD_LEN, D), x.dtype),
                         jnp.broadcast_to(meta.astype(x.dtype)[None], (B, N_META, D)), x], axis=1)
    pos = jnp.arange(T)
    valid = pos >= PAD_LEN
    rel = (pos - PAD_LEN).astype(jnp.float32)
    inv_freq = ROPE_BASE ** (-(jnp.arange(0, MLA_ROPE, 2, dtype=jnp.float32) / MLA_ROPE))
    ang = rel[:, None] * inv_freq[None, :]
    cos = jnp.cos(ang).astype(x.dtype)
    sin = jnp.sin(ang).astype(x.dtype)
    for l in range(DEPTH):
        h = h + 0.5 * swiglu(rmsnorm(h, ln_ffn1[l]), ffn1_wi[l], ffn1_wo[l])
        u = jnp.where(valid[None, :, None], rmsnorm(h, ln_mix[l]), 0)
        h = h + hybrid_mixer(u, valid, cos, sin, w_in[l], fox_bf[l], mla_gq[l], mla_wq[l],
                             mla_gkv[l], mla_wkv[l], gdn_conv[l], gdn_alog[l], gdn_dtb[l],
                             gdn_gon[l], lru_conv[l], lru_conv_b[l], lru_wa[l], lru_ba[l],
                             lru_wx[l], lru_bx[l], lru_lam[l], w_gate[l], b_gate[l],
                             w_branch[l], w_out[l])
        h = h + 0.5 * swiglu(rmsnorm(h, ln_ffn2[l]), ffn2_wi[l], ffn2_wo[l])
    y = rmsnorm(h, ln_final)
    return y[:, BLOCK:]
```

```python
import functools
import math

import jax
import jax.numpy as jnp
from jax import lax
from jax.experimental import pallas as pl
from jax.experimental.pallas import tpu as pltpu

F32 = jnp.float32
BF16 = jnp.bfloat16

N_META = 16
BLOCK = 128
PAD_LEN = BLOCK - N_META
EPS = 1e-6
NEG_INF = -1e30
N_BRANCH = 4
D_BRANCH = 256
HEADS = 4
DH = 64
MLA_NOPE = 64
MLA_ROPE = 32
MLA_Q_RANK = 192
MLA_KV_RANK = 128
ROPE_BASE = 10000.0
GDN_CONV = 4
GDN_CHUNK = 64
LRU_CONV = 4
LRU_C = 8.0
HALO = 8

OFF_FOX_QKV = 0
OFF_FOX_F = OFF_FOX_QKV + 3 * HEADS * DH
OFF_MLA_CQ = OFF_FOX_F + HEADS
OFF_MLA_CKV = OFF_MLA_CQ + MLA_Q_RANK
OFF_MLA_KR = OFF_MLA_CKV + MLA_KV_RANK
OFF_GDN_QKV = OFF_MLA_KR + MLA_ROPE
OFF_GDN_A = OFF_GDN_QKV + HEADS * 3 * DH
OFF_GDN_B = OFF_GDN_A + HEADS
OFF_GDN_G = OFF_GDN_B + HEADS
OFF_LRU = OFF_GDN_G + HEADS * DH
N_IN = OFF_LRU + D_BRANCH

VMEM_LIMIT = 56 * 1024 * 1024


def _cparams(*sem):
    return pltpu.CompilerParams(dimension_semantics=sem, vmem_limit_bytes=VMEM_LIMIT)


def _resident(shape):
    nd = len(shape)
    return pl.BlockSpec(shape, lambda *_: (0,) * nd, pipeline_mode=pl.Buffered(1))


def _dot(a, b):
    return jnp.dot(a, b, preferred_element_type=F32)


def _dot_nt(a, b):
    return lax.dot_general(a, b, (((1,), (1,)), ((), ())), preferred_element_type=F32)


def _dot_tn(a, b):
    return lax.dot_general(a, b, (((0,), (0,)), ((), ())), preferred_element_type=F32)


def _split3(x):
    hi = x.astype(BF16)
    r1 = x - hi.astype(F32)
    mid = r1.astype(BF16)
    lo = (r1 - mid.astype(F32)).astype(BF16)
    return hi, mid, lo


def _dot_exact_rhs(x, m):
    hi, mid, lo = _split3(x)
    return _dot(hi, m) + _dot(mid, m) + _dot(lo, m)


def _dot_exact_lhs(m, x):
    hi, mid, lo = _split3(x)
    return _dot(m, hi) + _dot(m, mid) + _dot(m, lo)


def _rms(x, g):
    var = jnp.mean(x * x, axis=-1, keepdims=True)
    return x * lax.rsqrt(var + EPS) * g


def _sigmoid(x):
    return 1.0 / (1.0 + jnp.exp(-x))


def _softplus(x):
    return jnp.maximum(x, 0.0) + jnp.log1p(jnp.exp(-jnp.abs(x)))


def _iota(shape, dim):
    return lax.broadcasted_iota(jnp.int32, shape, dim)


def _row_tile(t):
    for cand in (384, 256, 512, 128):
        if t % cand == 0:
            return cand
    raise ValueError(f"T={t} has no supported row tile")


def _ffn_kernel(x_ref, g_ref, wi_ref, wo_ref, gf_ref, o_ref, *, d_ff, final_norm):
    x = x_ref[...]
    xn = _rms(x, g_ref[...]).astype(BF16)
    gu = _dot(xn, wi_ref[...])
    g = gu[:, :d_ff]
    u = gu[:, d_ff:]
    act = (g * _sigmoid(g) * u).astype(BF16)
    y = x + 0.5 * _dot(act, wo_ref[...])
    if final_norm:
        y = _rms(y, gf_ref[...])
    o_ref[...] = y


def _ffn(h2, g, wi, wo, gf, *, final_norm, tm):
    n, d = h2.shape
    d_ff = wo.shape[0]
    return pl.pallas_call(
        functools.partial(_ffn_kernel, d_ff=d_ff, final_norm=final_norm),
        grid=(n // tm,),
        in_specs=[pl.BlockSpec((tm, d), lambda i: (i, 0)),
                  _resident((1, d)), _resident(wi.shape), _resident(wo.shape), _resident((1, d))],
        out_specs=pl.BlockSpec((tm, d), lambda i: (i, 0)),
        out_shape=jax.ShapeDtypeStruct((n, d), F32),
        compiler_params=_cparams("parallel"),
        name="ffn",
    )(h2, g, wi, wo, gf)


def _inproj_kernel(h_ref, g_ref, wfox_ref, wft_ref, bf_ref, wmla_ref, gq_ref, wq_ref, wqs_ref,
                   gkv_ref, wkv_ref, cos_ref, sin_ref, wgdn_ref, wlru_ref, tri_ref,
                   fq_ref, fk_ref, fv_ref, fcum_ref, mq_ref, mk_ref, mv_ref,
                   gqkv_ref, gab_ref, ggate_ref, lru_ref, carry_ref, *, tt):
    t = pl.program_id(1)
    rows = t * tt + _iota((tt, 1), 0)
    u = jnp.where(rows >= PAD_LEN, _rms(h_ref[...], g_ref[...]), 0.0).astype(BF16)

    w = HEADS * DH
    fox = _dot(u, wfox_ref[...])
    fq_ref[...] = (fox[:, :w] * (DH ** -0.5)).astype(BF16)
    fk_ref[...] = fox[:, w:2 * w].astype(BF16)
    fv_ref[...] = fox[:, 2 * w:].astype(BF16)

    z = _dot_nt(wft_ref[...], u) + bf_ref[...]
    log_f = jnp.minimum(z, 0.0) - jnp.log1p(jnp.exp(-jnp.abs(z)))

    @pl.when(t == 0)
    def _():
        carry_ref[...] = jnp.zeros_like(carry_ref)

    cum = _dot_exact_rhs(log_f, tri_ref[...]) + carry_ref[...]
    fcum_ref[...] = cum
    carry_ref[...] = cum[:, tt - 1:tt]

    pm = _dot(u, wmla_ref[...])
    cos = cos_ref[...]
    sin = sin_ref[...]
    cq = pm[:, :256]
    cqn = (cq * lax.rsqrt(jnp.sum(cq * cq, axis=-1, keepdims=True) / MLA_Q_RANK + EPS)
           * gq_ref[...]).astype(BF16)
    qa = _dot(cqn, wq_ref[...])
    qb = _dot(cqn, wqs_ref[...])
    ckv = pm[:, 256:384]
    ckvn = (_rms(ckv, gkv_ref[...])).astype(BF16)
    kv = _dot(ckvn, wkv_ref[...])
    kr = pm[:, 384:512] * cos + pm[:, 512:640] * sin
    scale = (MLA_NOPE + MLA_ROPE) ** -0.5
    for hd in range(HEADS):
        sl = slice(hd * 128, (hd + 1) * 128)
        mq_ref[:, sl] = ((qa[:, sl] * cos + qb[:, sl] * sin) * scale).astype(BF16)
        mk_ref[:, sl] = (kv[:, sl] + kr).astype(BF16)
    mv_ref[...] = kv[:, HEADS * 128:].astype(BF16)

    pg = _dot(u, wgdn_ref[...])
    gqkv_ref[...] = pg[:, :3 * w]
    gab_ref[...] = pg[:, 3 * w:5 * w]
    ggate_ref[...] = pg[:, 5 * w:]
    lru_ref[...] = _dot(u, wlru_ref[...])


def _inproj(h, lw, cos128, sin128, tri, *, tt):
    b, t, d = h.shape
    w = HEADS * DH
    row = lambda c: pl.BlockSpec((None, tt, c), lambda i, j: (i, j, 0))
    ins = [h, lw["ln_mix"], lw["w_fox"], lw["w_ft"], lw["b_f"], lw["w_mla"], lw["g_q"], lw["w_q"],
           lw["w_qs"], lw["g_kv"], lw["w_kv"], cos128, sin128, lw["w_gdn"], lw["w_lru"], tri]
    in_specs = [row(d)] + [_resident(a.shape) for a in ins[1:11]]
    in_specs += [pl.BlockSpec((tt, 128), lambda i, j: (j, 0))] * 2
    in_specs += [_resident(a.shape) for a in ins[13:]]
    out_shape = [jax.ShapeDtypeStruct((b, t, w), BF16)] * 3
    out_shape += [jax.ShapeDtypeStruct((b, 8, t), F32)]
    out_shape += [jax.ShapeDtypeStruct((b, t, 512), BF16)] * 2 + [jax.ShapeDtypeStruct((b, t, w), BF16)]
    out_shape += [jax.ShapeDtypeStruct((b, t, 3 * w), F32), jax.ShapeDtypeStruct((b, t, 2 * w), F32),
                  jax.ShapeDtypeStruct((b, t, w), F32), jax.ShapeDtypeStruct((b, t, w), F32)]
    out_specs = [row(w)] * 3 + [pl.BlockSpec((None, 8, tt), lambda i, j: (i, 0, j))]
    out_specs += [row(512)] * 2 + [row(w), row(3 * w), row(2 * w), row(w), row(w)]
    return pl.pallas_call(
        functools.partial(_inproj_kernel, tt=tt),
        grid=(b, t // tt),
        in_specs=in_specs, out_specs=out_specs, out_shape=out_shape,
        scratch_shapes=[pltpu.VMEM((8, 1), F32)],
        compiler_params=_cparams("parallel", "arbitrary"),
        name="inproj",
    )(*ins)


def _attn_kernel(*refs, tq, tk, head_slices, has_bias):
    if has_bias:
        q_ref, k_ref, v_ref, bias_ref, o_ref, m_ref, l_ref, acc_ref = refs
    else:
        q_ref, k_ref, v_ref, o_ref, m_ref, l_ref, acc_ref = refs
        bias_ref = None
    qi = pl.program_id(1)
    q0 = qi * tq
    wv = v_ref.shape[-1]
    lane_head = _iota((1, wv), 1) // DH
    qpos = q0 + _iota((tq, tk), 0)
    kidx = _iota((tq, tk), 1)
    n_kb = (q0 + tq + tk - 1) // tk
    out = jnp.zeros((tq, wv), F32)
    for hd in range(HEADS):
        if head_slices:
            qh = q_ref[:, hd * 128:(hd + 1) * 128]
        else:
            q = q_ref[...]
            qh = jnp.where(_iota((1, q.shape[-1]), 1) // DH == hd, q, jnp.zeros_like(q))
        m_ref[...] = jnp.full_like(m_ref, NEG_INF)
        l_ref[...] = jnp.zeros_like(l_ref)
        acc_ref[...] = jnp.zeros_like(acc_ref)

        def body(kb, carry, qh=qh, hd=hd):
            k0 = pl.multiple_of(kb * tk, tk)
            if head_slices:
                kblk = k_ref[pl.ds(k0, tk), hd * 128:(hd + 1) * 128]
            else:
                kblk = k_ref[pl.ds(k0, tk), :]
            s = _dot_nt(qh, kblk)
            if has_bias:
                s = s - bias_ref[hd:hd + 1, pl.ds(k0, tk)]
            kpos = k0 + kidx
            s = jnp.where((kpos <= qpos) & (kpos >= PAD_LEN), s, NEG_INF)
            m_old = m_ref[...]
            m_new = jnp.maximum(m_old, jnp.max(s, axis=-1, keepdims=True))
            alpha = jnp.exp(m_old - m_new)
            p = jnp.exp(s - m_new)
            l_ref[...] = alpha * l_ref[...] + jnp.sum(p, axis=-1, keepdims=True)
            acc_ref[...] = alpha * acc_ref[...] + _dot(p.astype(BF16), v_ref[pl.ds(k0, tk), :])
            m_ref[...] = m_new
            return carry

        lax.fori_loop(0, n_kb, body, 0)
        out = jnp.where(lane_head == hd, acc_ref[...] / l_ref[...], out)
    o_ref[...] = out.astype(o_ref.dtype)


def _attention(q, k, v, bias, *, head_slices, tq, tk):
    b, t, wq = q.shape
    wk = k.shape[-1]
    wv = v.shape[-1]
    ins = [q, k, v]
    in_specs = [pl.BlockSpec((None, tq, wq), lambda i, j: (i, j, 0)),
                pl.BlockSpec((None, t, wk), lambda i, j: (i, 0, 0)),
                pl.BlockSpec((None, t, wv), lambda i, j: (i, 0, 0))]
    if bias is not None:
        ins.append(bias)
        in_specs.append(pl.BlockSpec((None, 8, t), lambda i, j: (i, 0, 0)))
    return pl.pallas_call(
        functools.partial(_attn_kernel, tq=tq, tk=tk, head_slices=head_slices,
                          has_bias=bias is not None),
        grid=(b, t // tq),
        in_specs=in_specs,
        out_specs=pl.BlockSpec((None, tq, wv), lambda i, j: (i, j, 0)),
        out_shape=jax.ShapeDtypeStruct((b, t, wv), BF16),
        scratch_shapes=[pltpu.VMEM((tq, 1), F32), pltpu.VMEM((tq, 1), F32),
                        pltpu.VMEM((tq, wv), F32)],
        compiler_params=_cparams("parallel", "arbitrary"),
        name="attn_mla" if head_slices else "attn_fox",
    )(*ins)


def _gdn_kernel(qkv_ref, ab_ref, gate_ref, cw_ref, alog_ref, dtb_ref, gon_ref, o_ref,
                ext_ref, q_s, k_s, v_s, g_s, b_s, o_s, state_ref, *, tg):
    t = pl.program_id(1)
    c = GDN_CHUNK
    w = HEADS * DH

    @pl.when(t == 0)
    def _():
        ext_ref[0:HALO, :] = jnp.zeros((HALO, 3 * w), F32)
        state_ref[...] = jnp.zeros_like(state_ref)

    @pl.when(t > 0)
    def _():
        ext_ref[0:HALO, :] = ext_ref[tg:tg + HALO, :]

    ext_ref[HALO:HALO + tg, :] = qkv_ref[...]
    conv = jnp.zeros((tg, 3 * w), F32)
    for kk in range(GDN_CONV):
        off = HALO - (GDN_CONV - 1) + kk
        conv = conv + cw_ref[kk:kk + 1, :] * ext_ref[off:off + tg, :]
    qkv = conv * _sigmoid(conv)

    bd = (_iota((w, w), 0) // DH) == (_iota((w, w), 1) // DH)
    bd_ones = jnp.where(bd, 1.0, 0.0).astype(BF16)

    def l2n(x):
        return x * lax.rsqrt(_dot_exact_rhs(x * x, bd_ones) + EPS)

    q_s[...] = l2n(qkv[:, :w]) * (DH ** -0.5)
    k_s[...] = l2n(qkv[:, w:2 * w])
    v_s[...] = qkv[:, 2 * w:]
    ab = ab_ref[...]
    g_s[...] = -jnp.exp(alog_ref[...]) * _softplus(ab[:, :w] + dtb_ref[...])
    b_s[...] = _sigmoid(ab[:, w:])

    row = _iota((c, w), 0)
    col = _iota((c, w), 1) % DH
    eye_rep = row == col
    incl_rep = row >= col
    strict_rep = row > col
    ltri = jnp.where(_iota((c, c), 0) >= _iota((c, c), 1), 1.0, 0.0).astype(BF16)
    tile4 = jnp.where(_iota((c, w), 0) == col, 1.0, 0.0).astype(BF16)

    def zmat(x):
        xb = x.astype(BF16)
        return jnp.where(bd, jnp.concatenate([xb] * HEADS, axis=0), jnp.zeros((w, w), BF16))

    def mm(x, z):
        return _dot(x.astype(BF16), z)

    def chunk(ci, carry):
        r0 = pl.multiple_of(ci * c, c)
        qc = q_s[pl.ds(r0, c), :]
        kc = k_s[pl.ds(r0, c), :]
        vc = v_s[pl.ds(r0, c), :]
        bc = b_s[pl.ds(r0, c), :]
        big_g = _dot_exact_lhs(ltri, g_s[pl.ds(r0, c), :])
        g_row = jnp.sum(jnp.where(eye_rep, big_g, 0.0), axis=0, keepdims=True)
        decay = jnp.exp(jnp.where(incl_rep, big_g - g_row, NEG_INF))
        e_g = jnp.exp(big_g)
        g_last = big_g[c - 1:c, :]
        kb = kc * bc
        vb = vc * bc
        zk = jnp.where(bd, _dot_tn(kc.astype(BF16), tile4), 0.0).astype(BF16)
        a = jnp.where(strict_rep, mm(kb, zk) * decay, 0.0)
        qk = jnp.where(incl_rep, mm(qc, zk) * decay, 0.0)
        p = -a
        tinv = jnp.where(eye_rep, 1.0, 0.0) + p
        for _ in range(int(math.log2(c)) - 1):
            p = mm(p, zmat(p))
            tinv = tinv + mm(tinv, zmat(p))
        wmat = mm(tinv, zmat(kb * e_g))
        umat = mm(tinv, zmat(vb))
        q_dec = qc * e_g
        k_dec = kc * jnp.exp(g_last - big_g)
        s_bf = state_ref[...].astype(BF16)
        ws_qs = _dot(jnp.concatenate([wmat, q_dec], axis=0).astype(BF16), s_bf)
        v_new = umat - ws_qs[:c]
        o_s[pl.ds(r0, c), :] = ws_qs[c:] + mm(qk, zmat(v_new))
        upd = _dot_tn(k_dec.astype(BF16), v_new.astype(BF16))
        state_ref[...] = state_ref[...] * jnp.exp(g_last) + jnp.where(bd, upd, 0.0)
        return carry

    lax.fori_loop(0, tg // c, chunk, 0)

    o = o_s[...]
    ms = _dot_exact_rhs(o * o, bd_ones) * (1.0 / DH)
    gate = gate_ref[...]
    o_ref[...] = (o * lax.rsqrt(ms + EPS) * gon_ref[...] * (gate * _sigmoid(gate))).astype(o_ref.dtype)


def _gdn(gqkv, gab, ggate, lw, *, tg):
    b, t, _ = gqkv.shape
    w = HEADS * DH
    row = lambda c: pl.BlockSpec((None, tg, c), lambda i, j: (i, j, 0))
    vm = lambda c: pltpu.VMEM((tg, c), F32)
    return pl.pallas_call(
        functools.partial(_gdn_kernel, tg=tg),
        grid=(b, t // tg),
        in_specs=[row(3 * w), row(2 * w), row(w), _resident((GDN_CONV, 3 * w)),
                  _resident((1, w)), _resident((1, w)), _resident((1, w))],
        out_specs=row(w),
        out_shape=jax.ShapeDtypeStruct((b, t, w), BF16),
        scratch_shapes=[pltpu.VMEM((tg + HALO, 3 * w), F32), vm(w), vm(w), vm(w), vm(w), vm(w), vm(w),
                        pltpu.VMEM((w, w), F32)],
        compiler_params=_cparams("parallel", "arbitrary"),
        name="gdn",
    )(gqkv, gab, ggate, lw["gdn_conv"], lw["gdn_alog"], lw["gdn_dtb"], lw["gdn_gon"])


def _lru_kernel(x_ref, cw_ref, cb_ref, wa_ref, ba_ref, wx_ref, bx_ref, lam_ref, o_ref,
                ext_ref, h_ref, *, tl):
    t = pl.program_id(1)
    w = D_BRANCH

    @pl.when(t == 0)
    def _():
        ext_ref[0:HALO, :] = jnp.zeros((HALO, w), F32)
        h_ref[...] = jnp.zeros_like(h_ref)

    @pl.when(t > 0)
    def _():
        ext_ref[0:HALO, :] = ext_ref[tl:tl + HALO, :]

    ext_ref[HALO:HALO + tl, :] = x_ref[...]
    xr = jnp.zeros((tl, w), F32) + cb_ref[...]
    for kk in range(LRU_CONV):
        off = HALO - (LRU_CONV - 1) + kk
        xr = xr + cw_ref[kk:kk + 1, :] * ext_ref[off:off + tl, :]
    rows = _iota((tl, 1), 0)
    xr = jnp.where(t * tl + rows >= PAD_LEN, xr, 0.0)
    xb = xr.astype(BF16)
    r = _sigmoid(_dot(xb, wa_ref[...]) + ba_ref[...])
    ig = _sigmoid(_dot(xb, wx_ref[...]) + bx_ref[...])
    log_a = -LRU_C * r * _softplus(-lam_ref[...])
    a = jnp.exp(log_a)
    bv = jnp.sqrt(-jnp.tanh(log_a) * (a * a + 1.0)) * ig * xr
    s = 1
    while s < tl:
        keep = rows >= s
        a_sh = jnp.where(keep, pltpu.roll(a, s, 0), 1.0)
        b_sh = jnp.where(keep, pltpu.roll(bv, s, 0), 0.0)
        bv = a * b_sh + bv
        a = a * a_sh
        s *= 2
    h = a * h_ref[...] + bv
    h_ref[...] = h[tl - 1:tl, :]
    o_ref[...] = h.astype(o_ref.dtype)


def _lru(x, lw, *, tl):
    b, t, w = x.shape
    row = pl.BlockSpec((None, tl, w), lambda i, j: (i, j, 0))
    return pl.pallas_call(
        functools.partial(_lru_kernel, tl=tl),
        grid=(b, t // tl),
        in_specs=[row, _resident((LRU_CONV, w)), _resident((1, w)), _resident((w, w)),
                  _resident((1, w)), _resident((w, w)), _resident((1, w)), _resident((1, w))],
        out_specs=row,
        out_shape=jax.ShapeDtypeStruct((b, t, w), BF16),
        scratch_shapes=[pltpu.VMEM((tl + HALO, w), F32), pltpu.VMEM((1, w), F32)],
        compiler_params=_cparams("parallel", "arbitrary"),
        name="lru",
    )(x, lw["lru_conv"], lw["lru_conv_b"], lw["lru_wa"], lw["lru_ba"], lw["lru_wx"],
      lw["lru_bx"], lw["lru_lam"])


def _merge_kernel(h_ref, y0_ref, y1_ref, y2_ref, y3_ref, g_ref, wg_ref, bg_ref, wb_ref, wo_ref,
                  o_ref, *, tt):
    t = pl.program_id(1)
    h = h_ref[...]
    rows = t * tt + _iota((tt, 1), 0)
    u = jnp.where(rows >= PAD_LEN, _rms(h, g_ref[...]), 0.0).astype(BF16)
    merged = None
    for n, y_ref in enumerate((y0_ref, y1_ref, y2_ref, y3_ref)):
        gate = _sigmoid(_dot(u, wg_ref[n]) + bg_ref[n])
        term = gate * _dot(y_ref[...], wb_ref[n])
        merged = term if merged is None else merged + term
    o_ref[...] = h + _dot(merged.astype(BF16), wo_ref[...])


def _merge(h, ys, lw, *, tt):
    b, t, d = h.shape
    row = lambda c: pl.BlockSpec((None, tt, c), lambda i, j: (i, j, 0))
    return pl.pallas_call(
        functools.partial(_merge_kernel, tt=tt),
        grid=(b, t // tt),
        in_specs=[row(d)] + [row(D_BRANCH)] * N_BRANCH + [
            _resident((1, d)), _resident(lw["w_gate"].shape), _resident(lw["b_gate"].shape),
            _resident(lw["w_branch"].shape), _resident((d, d))],
        out_specs=row(d),
        out_shape=jax.ShapeDtypeStruct((b, t, d), F32),
        compiler_params=_cparams("parallel", "parallel"),
        name="merge",
    )(h, *ys, lw["ln_mix"], lw["w_gate"], lw["b_gate"], lw["w_branch"], lw["w_out"])


def _rope_swap(wr):
    half = wr.shape[-1] // 2
    return jnp.concatenate([-wr[..., half:], wr[..., :half]], axis=-1)


def _head_blocks(parts, width=128):
    rows = parts[0].shape[0]
    x = jnp.concatenate(parts, axis=-1)
    x = jnp.pad(x, ((0, 0), (0, 0), (0, width - x.shape[-1])))
    return x.reshape(rows, HEADS * width)


def _layer_weights(l, w_in, fox_bf, mla_gq, mla_wq, mla_gkv, mla_wkv, gdn_conv, gdn_alog, gdn_dtb,
                   gdn_gon, lru_conv, lru_conv_b, lru_wa, lru_ba, lru_wx, lru_bx, lru_lam,
                   w_gate, b_gate, w_branch, w_out, ln_mix):
    d = w_in.shape[1]
    wi = w_in[l]
    w = HEADS * DH
    lw = {"ln_mix": ln_mix[l][None]}
    lw["w_fox"] = wi[:, OFF_FOX_QKV:OFF_FOX_F].astype(BF16)
    lw["w_ft"] = jnp.pad(wi[:, OFF_FOX_F:OFF_MLA_CQ].T, ((0, 8 - HEADS), (0, 0))).astype(BF16)
    lw["b_f"] = jnp.pad(fox_bf[l], (0, 8 - HEADS))[:, None]
    z64 = jnp.zeros((d, 64), F32)
    z32 = jnp.zeros((d, 32), F32)
    kr = wi[:, OFF_MLA_KR:OFF_GDN_QKV]
    lw["w_mla"] = jnp.concatenate(
        [wi[:, OFF_MLA_CQ:OFF_MLA_CKV], z64, wi[:, OFF_MLA_CKV:OFF_MLA_KR],
         z64, kr, z32, z64, _rope_swap(kr), z32], axis=-1).astype(BF16)
    lw["g_q"] = jnp.pad(mla_gq[l], (0, 256 - MLA_Q_RANK))[None]
    wq = mla_wq[l].reshape(MLA_Q_RANK, HEADS, MLA_NOPE + MLA_ROPE)
    wq_nope, wq_rope = wq[..., :MLA_NOPE], wq[..., MLA_NOPE:]
    padq = lambda x: jnp.pad(x, ((0, 256 - MLA_Q_RANK), (0, 0))).astype(BF16)
    lw["w_q"] = padq(_head_blocks([wq_nope, wq_rope]))
    lw["w_qs"] = padq(_head_blocks([jnp.zeros_like(wq_nope), _rope_swap(wq_rope)]))
    lw["g_kv"] = mla_gkv[l][None]
    wkv = mla_wkv[l].reshape(MLA_KV_RANK, HEADS, MLA_NOPE + DH)
    lw["w_kv"] = jnp.concatenate(
        [_head_blocks([wkv[..., :MLA_NOPE]]), wkv[..., MLA_NOPE:].reshape(MLA_KV_RANK, w)],
        axis=-1).astype(BF16)
    rep = lambda x: jnp.repeat(x, DH, axis=-1)
    lw["w_gdn"] = jnp.concatenate(
        [wi[:, OFF_GDN_QKV:OFF_GDN_A], rep(wi[:, OFF_GDN_A:OFF_GDN_B]),
         rep(wi[:, OFF_GDN_B:OFF_GDN_G]), wi[:, OFF_GDN_G:OFF_LRU]], axis=-1).astype(BF16)
    lw["gdn_conv"] = gdn_conv[l]
    lw["gdn_alog"] = rep(gdn_alog[l])[None]
    lw["gdn_dtb"] = rep(gdn_dtb[l])[None]
    lw["gdn_gon"] = jnp.tile(gdn_gon[l], HEADS)[None]
    lw["w_lru"] = wi[:, OFF_LRU:N_IN].astype(BF16)
    lw["lru_conv"] = lru_conv[l]
    lw["lru_conv_b"] = lru_conv_b[l][None]
    lw["lru_wa"] = jax.scipy.linalg.block_diag(*lru_wa[l]).astype(BF16)
    lw["lru_ba"] = lru_ba[l][None]
    lw["lru_wx"] = jax.scipy.linalg.block_diag(*lru_wx[l]).astype(BF16)
    lw["lru_bx"] = lru_bx[l][None]
    lw["lru_lam"] = lru_lam[l][None]
    lw["w_gate"] = w_gate[l].astype(BF16)
    lw["b_gate"] = b_gate[l][:, None, :]
    lw["w_branch"] = w_branch[l].astype(BF16)
    lw["w_out"] = w_out[l].astype(BF16)
    return lw


def _rope_tables(t):
    rel = (jnp.arange(t) - PAD_LEN).astype(F32)
    inv_freq = ROPE_BASE ** (-(jnp.arange(0, MLA_ROPE, 2, dtype=F32) / MLA_ROPE))
    ang = rel[:, None] * inv_freq[None, :]
    cos, sin = jnp.cos(ang), jnp.sin(ang)
    pad = jnp.zeros((t, 128 - MLA_NOPE - MLA_ROPE), F32)
    cos128 = jnp.concatenate([jnp.ones((t, MLA_NOPE), F32), cos, cos, pad], axis=-1)
    sin128 = jnp.concatenate([jnp.zeros((t, MLA_NOPE), F32), sin, sin, pad], axis=-1)
    return cos128, sin128


def _ffn_rows(n):
    for cand in (512, 256, 128, 64, 32, 16, 8):
        if n % cand == 0:
            return cand
    raise ValueError(f"row count {n} is not a multiple of 8")


def kernel(x, meta, ln_ffn1, ffn1_wi, ffn1_wo, ln_mix, w_in, fox_bf, mla_gq, mla_wq, mla_gkv, mla_wkv, gdn_conv, gdn_alog, gdn_dtb, gdn_gon, lru_conv, lru_conv_b, lru_wa, lru_ba, lru_wx, lru_bx, lru_lam, w_gate, b_gate, w_branch, w_out, ln_ffn2, ffn2_wi, ffn2_wo, ln_final):
    b, s, d = x.shape
    t = BLOCK + s
    depth = w_in.shape[0]
    h = jnp.concatenate([jnp.zeros((b, PAD_LEN, d), x.dtype),
                         jnp.broadcast_to(meta.astype(x.dtype)[None], (b, N_META, d)), x], axis=1)
    tt = _row_tile(t)
    tm = _ffn_rows(b * t)
    cos128, sin128 = _rope_tables(t)
    tri = (jnp.arange(tt)[:, None] <= jnp.arange(tt)[None, :]).astype(BF16)
    gf = ln_final[None]
    for l in range(depth):
        lw = _layer_weights(l, w_in, fox_bf, mla_gq, mla_wq, mla_gkv, mla_wkv, gdn_conv, gdn_alog,
                            gdn_dtb, gdn_gon, lru_conv, lru_conv_b, lru_wa, lru_ba, lru_wx, lru_bx,
                            lru_lam, w_gate, b_gate, w_branch, w_out, ln_mix)
        h = _ffn(h.reshape(b * t, d), ln_ffn1[l][None], ffn1_wi[l].astype(BF16),
                 ffn1_wo[l].astype(BF16), gf, final_norm=False, tm=tm).reshape(b, t, d)
        (fq, fk, fv, fcum, mq, mk, mv, gqkv, gab, ggate, plru) = _inproj(
            h, lw, cos128, sin128, tri, tt=tt)
        y_fox = _attention(fq, fk, fv, fcum, head_slices=False, tq=tt, tk=tt)
        y_mla = _attention(mq, mk, mv, None, head_slices=True, tq=tt, tk=tt)
        y_gdn = _gdn(gqkv, gab, ggate, lw, tg=tt)
        y_lru = _lru(plru, lw, tl=tt)
        h = _merge(h, (y_fox, y_mla, y_gdn, y_lru), lw, tt=tt)
        h = _ffn(h.reshape(b * t, d), ln_ffn2[l][None], ffn2_wi[l].astype(BF16),
                 ffn2_wo[l].astype(BF16), gf, final_norm=(l == depth - 1), tm=tm).reshape(b, t, d)
    return h[:, BLOCK:]
```

```python
import functools
import math

import jax
import jax.numpy as jnp
from jax import lax
from jax.experimental import pallas as pl
from jax.experimental.pallas import tpu as pltpu

F32 = jnp.float32
BF16 = jnp.bfloat16

N_META = 16
BLOCK = 128
PAD_LEN = BLOCK - N_META
EPS = 1e-6
NEG_INF = -1e30
N_BRANCH = 4
D_BRANCH = 256
HEADS = 4
DH = 64
MLA_NOPE = 64
MLA_ROPE = 32
MLA_Q_RANK = 192
MLA_KV_RANK = 128
ROPE_BASE = 10000.0
GDN_CONV = 4
GDN_CHUNK = 64
LRU_CONV = 4
LRU_C = 8.0
HALO = 8
LOG2E = 1.4426950408889634

OFF_FOX_QKV = 0
OFF_FOX_F = OFF_FOX_QKV + 3 * HEADS * DH
OFF_MLA_CQ = OFF_FOX_F + HEADS
OFF_MLA_CKV = OFF_MLA_CQ + MLA_Q_RANK
OFF_MLA_KR = OFF_MLA_CKV + MLA_KV_RANK
OFF_GDN_QKV = OFF_MLA_KR + MLA_ROPE
OFF_GDN_A = OFF_GDN_QKV + HEADS * 3 * DH
OFF_GDN_B = OFF_GDN_A + HEADS
OFF_GDN_G = OFF_GDN_B + HEADS
OFF_LRU = OFF_GDN_G + HEADS * DH
N_IN = OFF_LRU + D_BRANCH

VMEM_LIMIT = 56 * 1024 * 1024


def _cparams(*sem):
    return pltpu.CompilerParams(dimension_semantics=sem, vmem_limit_bytes=VMEM_LIMIT)


def _resident(shape):
    nd = len(shape)
    return pl.BlockSpec(shape, lambda *_: (0,) * nd, pipeline_mode=pl.Buffered(1))


def _dot(a, b):
    return jnp.dot(a, b, preferred_element_type=F32)


def _dot_nt(a, b):
    return lax.dot_general(a, b, (((1,), (1,)), ((), ())), preferred_element_type=F32)


def _dot_tn(a, b):
    return lax.dot_general(a, b, (((0,), (0,)), ((), ())), preferred_element_type=F32)


def _split3(x):
    hi = x.astype(BF16)
    r1 = x - hi.astype(F32)
    mid = r1.astype(BF16)
    lo = (r1 - mid.astype(F32)).astype(BF16)
    return hi, mid, lo


def _dot_exact_rhs(x, m):
    hi, mid, lo = _split3(x)
    return _dot(hi, m) + _dot(mid, m) + _dot(lo, m)


def _dot_exact_lhs(m, x):
    hi, mid, lo = _split3(x)
    return _dot(m, hi) + _dot(m, mid) + _dot(m, lo)


def _rms(x, g):
    var = jnp.mean(x * x, axis=-1, keepdims=True)
    return x * lax.rsqrt(var + EPS) * g


def _sigmoid(x):
    return 1.0 / (1.0 + jnp.exp(-x))


def _softplus(x):
    return jnp.maximum(x, 0.0) + jnp.log1p(jnp.exp(-jnp.abs(x)))


def _iota(shape, dim):
    return lax.broadcasted_iota(jnp.int32, shape, dim)


def _row_tile(t):
    for cand in (384, 256, 512, 128):
        if t % cand == 0:
            return cand
    raise ValueError(f"T={t} has no supported row tile")


def _ffn_kernel(x_ref, g_ref, wi_ref, wo_ref, gf_ref, o_ref, *, d_ff, final_norm):
    x = x_ref[...]
    xn = _rms(x, g_ref[...]).astype(BF16)
    gu = _dot(xn, wi_ref[...])
    g = gu[:, :d_ff]
    u = gu[:, d_ff:]
    act = (g * _sigmoid(g) * u).astype(BF16)
    y = x + 0.5 * _dot(act, wo_ref[...])
    if final_norm:
        y = _rms(y, gf_ref[...])
    o_ref[...] = y


def _ffn(h2, g, wi, wo, gf, *, final_norm, tm):
    n, d = h2.shape
    d_ff = wo.shape[0]
    return pl.pallas_call(
        functools.partial(_ffn_kernel, d_ff=d_ff, final_norm=final_norm),
        grid=(n // tm,),
        in_specs=[pl.BlockSpec((tm, d), lambda i: (i, 0)),
                  _resident((1, d)), _resident(wi.shape), _resident(wo.shape), _resident((1, d))],
        out_specs=pl.BlockSpec((tm, d), lambda i: (i, 0)),
        out_shape=jax.ShapeDtypeStruct((n, d), F32),
        compiler_params=_cparams("parallel"),
        name="ffn",
    )(h2, g, wi, wo, gf)


def _inproj_kernel(h_ref, g_ref, wfox_ref, wft_ref, bf_ref, wmla_ref, gq_ref, wq_ref, wqs_ref,
                   gkv_ref, wkv_ref, cos_ref, sin_ref, wgdn_ref, wlru_ref, tri_ref,
                   fq_ref, fk_ref, fv_ref, fcum_ref, mq_ref, mk_ref, mv_ref,
                   gqkv_ref, gab_ref, ggate_ref, lru_ref, carry_ref, *, tt):
    t = pl.program_id(1)
    rows = t * tt + _iota((tt, 1), 0)
    u = jnp.where(rows >= PAD_LEN, _rms(h_ref[...], g_ref[...]), 0.0).astype(BF16)

    w = HEADS * DH
    fox = _dot(u, wfox_ref[...])
    fq_ref[...] = (fox[:, :w] * (DH ** -0.5 * LOG2E)).astype(BF16)
    fk_ref[...] = fox[:, w:2 * w].astype(BF16)
    fv_ref[...] = fox[:, 2 * w:].astype(BF16)

    z = _dot_nt(wft_ref[...], u) + bf_ref[...]
    log_f = jnp.minimum(z, 0.0) - jnp.log1p(jnp.exp(-jnp.abs(z)))

    @pl.when(t == 0)
    def _():
        carry_ref[...] = jnp.zeros_like(carry_ref)

    cum = _dot_exact_rhs(log_f, tri_ref[...]) + carry_ref[...]
    fcum_ref[...] = cum * LOG2E
    carry_ref[...] = cum[:, tt - 1:tt]

    pm = _dot(u, wmla_ref[...])
    cos = cos_ref[...]
    sin = sin_ref[...]
    cq = pm[:, :256]
    cqn = (cq * lax.rsqrt(jnp.sum(cq * cq, axis=-1, keepdims=True) / MLA_Q_RANK + EPS)
           * gq_ref[...]).astype(BF16)
    qa = _dot(cqn, wq_ref[...])
    qb = _dot(cqn, wqs_ref[...])
    ckv = pm[:, 256:384]
    ckvn = (_rms(ckv, gkv_ref[...])).astype(BF16)
    kv = _dot(ckvn, wkv_ref[...])
    kr = pm[:, 384:512] * cos + pm[:, 512:640] * sin
    scale = (MLA_NOPE + MLA_ROPE) ** -0.5 * LOG2E
    for hd in range(HEADS):
        sl = slice(hd * 128, (hd + 1) * 128)
        mq_ref[:, sl] = ((qa[:, sl] * cos + qb[:, sl] * sin) * scale).astype(BF16)
        mk_ref[:, sl] = (kv[:, sl] + kr).astype(BF16)
    mv_ref[...] = kv[:, HEADS * 128:].astype(BF16)

    pg = _dot(u, wgdn_ref[...])
    gqkv_ref[...] = pg[:, :3 * w]
    gab_ref[...] = pg[:, 3 * w:5 * w]
    ggate_ref[...] = pg[:, 5 * w:]
    lru_ref[...] = _dot(u, wlru_ref[...])


def _inproj(h, lw, cos128, sin128, tri, *, tt):
    b, t, d = h.shape
    w = HEADS * DH
    row = lambda c: pl.BlockSpec((None, tt, c), lambda i, j: (i, j, 0))
    ins = [h, lw["ln_mix"], lw["w_fox"], lw["w_ft"], lw["b_f"], lw["w_mla"], lw["g_q"], lw["w_q"],
           lw["w_qs"], lw["g_kv"], lw["w_kv"], cos128, sin128, lw["w_gdn"], lw["w_lru"], tri]
    in_specs = [row(d)] + [_resident(a.shape) for a in ins[1:11]]
    in_specs += [pl.BlockSpec((tt, 128), lambda i, j: (j, 0))] * 2
    in_specs += [_resident(a.shape) for a in ins[13:]]
    out_shape = [jax.ShapeDtypeStruct((b, t, w), BF16)] * 3
    out_shape += [jax.ShapeDtypeStruct((b, 8, t), F32)]
    out_shape += [jax.ShapeDtypeStruct((b, t, 512), BF16)] * 2 + [jax.ShapeDtypeStruct((b, t, w), BF16)]
    out_shape += [jax.ShapeDtypeStruct((b, t, 3 * w), F32), jax.ShapeDtypeStruct((b, t, 2 * w), F32),
                  jax.ShapeDtypeStruct((b, t, w), F32), jax.ShapeDtypeStruct((b, t, w), F32)]
    out_specs = [row(w)] * 3 + [pl.BlockSpec((None, 8, tt), lambda i, j: (i, 0, j))]
    out_specs += [row(512)] * 2 + [row(w), row(3 * w), row(2 * w), row(w), row(w)]
    return pl.pallas_call(
        functools.partial(_inproj_kernel, tt=tt),
        grid=(b, t // tt),
        in_specs=in_specs, out_specs=out_specs, out_shape=out_shape,
        scratch_shapes=[pltpu.VMEM((8, 1), F32)],
        compiler_params=_cparams("parallel", "arbitrary"),
        name="inproj",
    )(*ins)


def _attn_kernel(*refs, tq, head_slices, has_bias):
    if has_bias:
        q_ref, k_ref, v_ref, bias_ref, o_ref, vp_ref, m_ref, acc_ref = refs
    else:
        q_ref, k_ref, v_ref, o_ref, vp_ref, m_ref, acc_ref = refs
        bias_ref = None
    t, wq = q_ref.shape
    wv = v_ref.shape[-1]
    half = wv // 2
    lane_head = _iota((1, wv), 1) // DH
    v = v_ref[...]
    for hd in range(HEADS):
        vp_ref[hd] = jnp.where(lane_head == hd, v, jnp.ones_like(v))

    def run_tile(q0, rows, steps):
        q = q_ref[pl.ds(q0, rows), :]
        if head_slices:
            qh = [q[:, hd * 128:(hd + 1) * 128] for hd in range(HEADS)]
        else:
            q_head = _iota((1, wq), 1) // DH
            qh = [jnp.where(q_head == hd, q, jnp.zeros_like(q)) for hd in range(HEADS)]
        for hd in range(HEADS):
            m_ref[hd, 0:rows, :] = jnp.full((rows, 128), NEG_INF, F32)
            acc_ref[hd, 0:rows, :] = jnp.zeros((rows, wv), F32)

        def step(k0, cols, mask):
            for hd in range(HEADS):
                if head_slices:
                    kblk = k_ref[pl.ds(k0, cols), hd * 128:(hd + 1) * 128]
                else:
                    kblk = k_ref[pl.ds(k0, cols), :]
                s = _dot_nt(qh[hd], kblk)
                if has_bias:
                    s = s - bias_ref[hd:hd + 1, pl.ds(k0, cols)]
                if mask is not None:
                    s = jnp.where(mask, s, NEG_INF)
                m_old = m_ref[hd, 0:rows, :]
                m_new = jnp.maximum(m_old, jnp.max(s, axis=-1, keepdims=True))
                alpha = jnp.exp2(m_old - m_new)
                p = jnp.exp2(s - jnp.concatenate([m_new] * (cols // 128), axis=1))
                pv = _dot(p.astype(BF16), vp_ref[hd, pl.ds(k0, cols), :])
                acc_ref[hd, 0:rows, :] = (acc_ref[hd, 0:rows, :]
                                          * jnp.concatenate([alpha] * (wv // 128), axis=1) + pv)
                m_ref[hd, 0:rows, :] = m_new

        steps(step)
        out = jnp.zeros((rows, wv), F32)
        for hd in range(HEADS):
            acc = acc_ref[hd, 0:rows, :]
            row_sum = jnp.concatenate([acc[:, half:], acc[:, :half]], axis=1)
            out = jnp.where(lane_head == hd, acc / row_sum, out)
        o_ref[pl.ds(q0, rows), :] = out.astype(o_ref.dtype)

    meta_keys = _iota((1, BLOCK), 1) >= PAD_LEN
    causal0 = (_iota((BLOCK, BLOCK), 1) <= _iota((BLOCK, BLOCK), 0)) & meta_keys
    run_tile(0, BLOCK, lambda step: step(0, BLOCK, causal0))

    causal = _iota((tq, tq), 1) <= _iota((tq, tq), 0)

    def outer(i, carry):
        q0 = pl.multiple_of(BLOCK + i * tq, 128)

        def steps(step):
            step(0, BLOCK, meta_keys)

            def full(j, c):
                step(pl.multiple_of(BLOCK + j * tq, 128), tq, None)
                return c

            lax.fori_loop(0, i, full, 0)
            step(q0, tq, causal)

        run_tile(q0, tq, steps)
        return carry

    lax.fori_loop(0, (t - BLOCK) // tq, outer, 0)


def _attention(q, k, v, bias, *, head_slices, tq):
    b, t, wq = q.shape
    wk = k.shape[-1]
    wv = v.shape[-1]
    assert wv == 2 * 128 and (t - BLOCK) % tq == 0
    full = lambda c: pl.BlockSpec((None, t, c), lambda i: (i, 0, 0))
    ins = [q, k, v]
    in_specs = [full(wq), full(wk), full(wv)]
    if bias is not None:
        ins.append(bias)
        in_specs.append(pl.BlockSpec((None, 8, t), lambda i: (i, 0, 0)))
    return pl.pallas_call(
        functools.partial(_attn_kernel, tq=tq, head_slices=head_slices, has_bias=bias is not None),
        grid=(b,),
        in_specs=in_specs,
        out_specs=full(wv),
        out_shape=jax.ShapeDtypeStruct((b, t, wv), BF16),
        scratch_shapes=[pltpu.VMEM((HEADS, t, wv), BF16), pltpu.VMEM((HEADS, tq, 128), F32),
                        pltpu.VMEM((HEADS, tq, wv), F32)],
        compiler_params=_cparams("parallel"),
        name="attn_mla" if head_slices else "attn_fox",
    )(*ins)


def _gdn_kernel(qkv_ref, ab_ref, gate_ref, cw_ref, alog_ref, dtb_ref, gon_ref, o_ref,
                ext_ref, q_s, k_s, v_s, g_s, b_s, o_s, state_ref, *, tg, group_chunks):
    t = pl.program_id(1)
    c = GDN_CHUNK
    w = HEADS * DH

    @pl.when(t == 0)
    def _():
        ext_ref[0:HALO, :] = jnp.zeros((HALO, 3 * w), F32)
        state_ref[...] = jnp.zeros_like(state_ref)

    @pl.when(t > 0)
    def _():
        ext_ref[0:HALO, :] = ext_ref[tg:tg + HALO, :]

    ext_ref[HALO:HALO + tg, :] = qkv_ref[...]
    x_ext = ext_ref[...]
    conv = cw_ref[GDN_CONV - 1:GDN_CONV, :] * x_ext[HALO:, :]
    for d in range(1, GDN_CONV):
        kk = GDN_CONV - 1 - d
        conv = conv + cw_ref[kk:kk + 1, :] * pltpu.roll(x_ext, d, 0)[HALO:, :]
    qkv = conv * _sigmoid(conv)

    bd = (_iota((w, w), 0) // DH) == (_iota((w, w), 1) // DH)
    bd_ones = jnp.where(bd, 1.0, 0.0).astype(BF16)

    def l2n(x):
        return x * lax.rsqrt(_dot_exact_rhs(x * x, bd_ones) + EPS)

    q_s[...] = l2n(qkv[:, :w]) * (DH ** -0.5)
    k_s[...] = l2n(qkv[:, w:2 * w])
    v_s[...] = qkv[:, 2 * w:]
    ab = ab_ref[...]
    g_s[...] = -jnp.exp(alog_ref[...]) * _softplus(ab[:, :w] + dtb_ref[...])
    b_s[...] = _sigmoid(ab[:, w:])

    row = _iota((c, w), 0)
    col = _iota((c, w), 1) % DH
    eye_rep = row == col
    incl_rep = row >= col
    strict_rep = row > col
    ltri = jnp.where(_iota((c, c), 0) >= _iota((c, c), 1), 1.0, 0.0).astype(BF16)
    tile4 = jnp.where(_iota((c, w), 0) == col, 1.0, 0.0).astype(BF16)

    def zmat(x):
        xb = x.astype(BF16)
        return jnp.where(bd, jnp.concatenate([xb] * HEADS, axis=0), jnp.zeros((w, w), BF16))

    def mm(x, z):
        return _dot(x.astype(BF16), z)

    def prep(r0s):
        each = lambda f, *xs: [f(*a) for a in zip(*xs)]
        qc = [q_s[pl.ds(r0, c), :] for r0 in r0s]
        kc = [k_s[pl.ds(r0, c), :] for r0 in r0s]
        bc = [b_s[pl.ds(r0, c), :] for r0 in r0s]
        kb = each(lambda k, b: k * b, kc, bc)
        vb = each(lambda r0, b: v_s[pl.ds(r0, c), :] * b, r0s, bc)
        big_g = [_dot_exact_lhs(ltri, g_s[pl.ds(r0, c), :]) for r0 in r0s]
        zk = [jnp.where(bd, _dot_tn(k.astype(BF16), tile4), 0.0).astype(BF16) for k in kc]
        kk = each(mm, kb, zk)
        qkr = each(mm, qc, zk)
        g_row = [jnp.sum(jnp.where(eye_rep, g, 0.0), axis=0, keepdims=True) for g in big_g]
        decay = each(lambda g, r: jnp.exp(jnp.where(incl_rep, g - r, NEG_INF)), big_g, g_row)
        qk = each(lambda x, d: jnp.where(incl_rep, x * d, 0.0), qkr, decay)
        p = each(lambda x, d: jnp.where(strict_rep, -(x * d), 0.0), kk, decay)
        tinv = [jnp.where(eye_rep, 1.0, 0.0) + x for x in p]
        for _ in range(int(math.log2(c)) - 1):
            p = [mm(x, zmat(x)) for x in p]
            tinv = each(lambda tv, x: tv + mm(tv, zmat(x)), tinv, p)
        e_g = [jnp.exp(g) for g in big_g]
        wmat = each(lambda tv, k, e: mm(tv, zmat(k * e)), tinv, kb, e_g)
        umat = each(lambda tv, x: mm(tv, zmat(x)), tinv, vb)
        wq = each(lambda wm, q, e: jnp.concatenate([wm, q * e], axis=0).astype(BF16), wmat, qc, e_g)
        k_dec = each(lambda k, g: (k * jnp.exp(g[c - 1:c, :] - g)).astype(BF16), kc, big_g)
        g_last = [jnp.exp(g[c - 1:c, :]) for g in big_g]
        return list(zip(wq, umat, qk, k_dec, g_last))

    def group(gi, carry):
        base = gi * (group_chunks * c)
        pre = prep([pl.multiple_of(base + j * c, c) for j in range(group_chunks)])
        state = state_ref[...]
        for j, (wq, umat, qk, k_dec, gl) in enumerate(pre):
            ws_qs = _dot(wq, state.astype(BF16))
            v_new = umat - ws_qs[:c]
            o_s[pl.ds(pl.multiple_of(base + j * c, c), c), :] = ws_qs[c:] + mm(qk, zmat(v_new))
            upd = _dot_tn(k_dec, v_new.astype(BF16))
            state = state * gl + jnp.where(bd, upd, 0.0)
        state_ref[...] = state
        return carry

    lax.fori_loop(0, tg // (group_chunks * c), group, 0)

    o = o_s[...]
    ms = _dot_exact_rhs(o * o, bd_ones) * (1.0 / DH)
    gate = gate_ref[...]
    o_ref[...] = (o * lax.rsqrt(ms + EPS) * gon_ref[...] * (gate * _sigmoid(gate))).astype(o_ref.dtype)


def _gdn(gqkv, gab, ggate, lw, *, tg):
    b, t, _ = gqkv.shape
    w = HEADS * DH
    row = lambda c: pl.BlockSpec((None, tg, c), lambda i, j: (i, j, 0))
    vm = lambda c: pltpu.VMEM((tg, c), F32)
    return pl.pallas_call(
        functools.partial(_gdn_kernel, tg=tg, group_chunks=next(
            g for g in (6, 4, 3, 2, 1) if (tg // GDN_CHUNK) % g == 0)),
        grid=(b, t // tg),
        in_specs=[row(3 * w), row(2 * w), row(w), _resident((GDN_CONV, 3 * w)),
                  _resident((1, w)), _resident((1, w)), _resident((1, w))],
        out_specs=row(w),
        out_shape=jax.ShapeDtypeStruct((b, t, w), BF16),
        scratch_shapes=[pltpu.VMEM((tg + HALO, 3 * w), F32), vm(w), vm(w), vm(w), vm(w), vm(w), vm(w),
                        pltpu.VMEM((w, w), F32)],
        compiler_params=_cparams("parallel", "arbitrary"),
        name="gdn",
    )(gqkv, gab, ggate, lw["gdn_conv"], lw["gdn_alog"], lw["gdn_dtb"], lw["gdn_gon"])


def _lru_kernel(x_ref, cw_ref, cb_ref, wa_ref, ba_ref, wx_ref, bx_ref, lam_ref, o_ref,
                ext_ref, h_ref, *, tl):
    t = pl.program_id(1)
    w = D_BRANCH

    @pl.when(t == 0)
    def _():
        ext_ref[0:HALO, :] = jnp.zeros((HALO, w), F32)
        h_ref[...] = jnp.zeros_like(h_ref)

    @pl.when(t > 0)
    def _():
        ext_ref[0:HALO, :] = ext_ref[tl:tl + HALO, :]

    ext_ref[HALO:HALO + tl, :] = x_ref[...]
    xr = jnp.zeros((tl, w), F32) + cb_ref[...]
    for kk in range(LRU_CONV):
        off = HALO - (LRU_CONV - 1) + kk
        xr = xr + cw_ref[kk:kk + 1, :] * ext_ref[off:off + tl, :]
    rows = _iota((tl, 1), 0)
    xr = jnp.where(t * tl + rows >= PAD_LEN, xr, 0.0)
    xb = xr.astype(BF16)
    r = _sigmoid(_dot(xb, wa_ref[...]) + ba_ref[...])
    ig = _sigmoid(_dot(xb, wx_ref[...]) + bx_ref[...])
    log_a = -LRU_C * r * _softplus(-lam_ref[...])
    a = jnp.exp(log_a)
    bv = jnp.sqrt(-jnp.tanh(log_a) * (a * a + 1.0)) * ig * xr
    s = 1
    while s < tl:
        keep = rows >= s
        a_sh = jnp.where(keep, pltpu.roll(a, s, 0), 1.0)
        b_sh = jnp.where(keep, pltpu.roll(bv, s, 0), 0.0)
        bv = a * b_sh + bv
        a = a * a_sh
        s *= 2
    h = a * h_ref[...] + bv
    h_ref[...] = h[tl - 1:tl, :]
    o_ref[...] = h.astype(o_ref.dtype)


def _lru(x, lw, *, tl):
    b, t, w = x.shape
    row = pl.BlockSpec((None, tl, w), lambda i, j: (i, j, 0))
    return pl.pallas_call(
        functools.partial(_lru_kernel, tl=tl),
        grid=(b, t // tl),
        in_specs=[row, _resident((LRU_CONV, w)), _resident((1, w)), _resident((w, w)),
                  _resident((1, w)), _resident((w, w)), _resident((1, w)), _resident((1, w))],
        out_specs=row,
        out_shape=jax.ShapeDtypeStruct((b, t, w), BF16),
        scratch_shapes=[pltpu.VMEM((tl + HALO, w), F32), pltpu.VMEM((1, w), F32)],
        compiler_params=_cparams("parallel", "arbitrary"),
        name="lru",
    )(x, lw["lru_conv"], lw["lru_conv_b"], lw["lru_wa"], lw["lru_ba"], lw["lru_wx"],
      lw["lru_bx"], lw["lru_lam"])


def _merge_kernel(h_ref, y0_ref, y1_ref, y2_ref, y3_ref, g_ref, wg_ref, bg_ref, wb_ref, wo_ref,
                  o_ref, *, tt):
    t = pl.program_id(1)
    h = h_ref[...]
    rows = t * tt + _iota((tt, 1), 0)
    u = jnp.where(rows >= PAD_LEN, _rms(h, g_ref[...]), 0.0).astype(BF16)
    merged = None
    for n, y_ref in enumerate((y0_ref, y1_ref, y2_ref, y3_ref)):
        gate = _sigmoid(_dot(u, wg_ref[n]) + bg_ref[n])
        term = gate * _dot(y_ref[...], wb_ref[n])
        merged = term if merged is None else merged + term
    o_ref[...] = h + _dot(merged.astype(BF16), wo_ref[...])


def _merge(h, ys, lw, *, tt):
    b, t, d = h.shape
    row = lambda c: pl.BlockSpec((None, tt, c), lambda i, j: (i, j, 0))
    return pl.pallas_call(
        functools.partial(_merge_kernel, tt=tt),
        grid=(b, t // tt),
        in_specs=[row(d)] + [row(D_BRANCH)] * N_BRANCH + [
            _resident((1, d)), _resident(lw["w_gate"].shape), _resident(lw["b_gate"].shape),
            _resident(lw["w_branch"].shape), _resident((d, d))],
        out_specs=row(d),
        out_shape=jax.ShapeDtypeStruct((b, t, d), F32),
        compiler_params=_cparams("parallel", "parallel"),
        name="merge",
    )(h, *ys, lw["ln_mix"], lw["w_gate"], lw["b_gate"], lw["w_branch"], lw["w_out"])


def _rope_swap(wr):
    half = wr.shape[-1] // 2
    return jnp.concatenate([-wr[..., half:], wr[..., :half]], axis=-1)


def _head_blocks(parts, width=128):
    rows = parts[0].shape[0]
    x = jnp.concatenate(parts, axis=-1)
    x = jnp.pad(x, ((0, 0), (0, 0), (0, width - x.shape[-1])))
    return x.reshape(rows, HEADS * width)


def _layer_weights(l, w_in, fox_bf, mla_gq, mla_wq, mla_gkv, mla_wkv, gdn_conv, gdn_alog, gdn_dtb,
                   gdn_gon, lru_conv, lru_conv_b, lru_wa, lru_ba, lru_wx, lru_bx, lru_lam,
                   w_gate, b_gate, w_branch, w_out, ln_mix):
    d = w_in.shape[1]
    wi = w_in[l]
    w = HEADS * DH
    lw = {"ln_mix": ln_mix[l][None]}
    lw["w_fox"] = wi[:, OFF_FOX_QKV:OFF_FOX_F].astype(BF16)
    lw["w_ft"] = jnp.pad(wi[:, OFF_FOX_F:OFF_MLA_CQ].T, ((0, 8 - HEADS), (0, 0))).astype(BF16)
    lw["b_f"] = jnp.pad(fox_bf[l], (0, 8 - HEADS))[:, None]
    z64 = jnp.zeros((d, 64), F32)
    z32 = jnp.zeros((d, 32), F32)
    kr = wi[:, OFF_MLA_KR:OFF_GDN_QKV]
    lw["w_mla"] = jnp.concatenate(
        [wi[:, OFF_MLA_CQ:OFF_MLA_CKV], z64, wi[:, OFF_MLA_CKV:OFF_MLA_KR],
         z64, kr, z32, z64, _rope_swap(kr), z32], axis=-1).astype(BF16)
    lw["g_q"] = jnp.pad(mla_gq[l], (0, 256 - MLA_Q_RANK))[None]
    wq = mla_wq[l].reshape(MLA_Q_RANK, HEADS, MLA_NOPE + MLA_ROPE)
    wq_nope, wq_rope = wq[..., :MLA_NOPE], wq[..., MLA_NOPE:]
    padq = lambda x: jnp.pad(x, ((0, 256 - MLA_Q_RANK), (0, 0))).astype(BF16)
    lw["w_q"] = padq(_head_blocks([wq_nope, wq_rope]))
    lw["w_qs"] = padq(_head_blocks([jnp.zeros_like(wq_nope), _rope_swap(wq_rope)]))
    lw["g_kv"] = mla_gkv[l][None]
    wkv = mla_wkv[l].reshape(MLA_KV_RANK, HEADS, MLA_NOPE + DH)
    lw["w_kv"] = jnp.concatenate(
        [_head_blocks([wkv[..., :MLA_NOPE]]), wkv[..., MLA_NOPE:].reshape(MLA_KV_RANK, w)],
        axis=-1).astype(BF16)
    rep = lambda x: jnp.repeat(x, DH, axis=-1)
    lw["w_gdn"] = jnp.concatenate(
        [wi[:, OFF_GDN_QKV:OFF_GDN_A], rep(wi[:, OFF_GDN_A:OFF_GDN_B]),
         rep(wi[:, OFF_GDN_B:OFF_GDN_G]), wi[:, OFF_GDN_G:OFF_LRU]], axis=-1).astype(BF16)
    lw["gdn_conv"] = gdn_conv[l]
    lw["gdn_alog"] = rep(gdn_alog[l])[None]
    lw["gdn_dtb"] = rep(gdn_dtb[l])[None]
    lw["gdn_gon"] = jnp.tile(gdn_gon[l], HEADS)[None]
    lw["w_lru"] = wi[:, OFF_LRU:N_IN].astype(BF16)
    lw["lru_conv"] = lru_conv[l]
    lw["lru_conv_b"] = lru_conv_b[l][None]
    lw["lru_wa"] = jax.scipy.linalg.block_diag(*lru_wa[l]).astype(BF16)
    lw["lru_ba"] = lru_ba[l][None]
    lw["lru_wx"] = jax.scipy.linalg.block_diag(*lru_wx[l]).astype(BF16)
    lw["lru_bx"] = lru_bx[l][None]
    lw["lru_lam"] = lru_lam[l][None]
    lw["w_gate"] = w_gate[l].astype(BF16)
    lw["b_gate"] = b_gate[l][:, None, :]
    lw["w_branch"] = w_branch[l].astype(BF16)
    lw["w_out"] = w_out[l].astype(BF16)
    return lw


def _rope_tables(t):
    rel = (jnp.arange(t) - PAD_LEN).astype(F32)
    inv_freq = ROPE_BASE ** (-(jnp.arange(0, MLA_ROPE, 2, dtype=F32) / MLA_ROPE))
    ang = rel[:, None] * inv_freq[None, :]
    cos, sin = jnp.cos(ang), jnp.sin(ang)
    pad = jnp.zeros((t, 128 - MLA_NOPE - MLA_ROPE), F32)
    cos128 = jnp.concatenate([jnp.ones((t, MLA_NOPE), F32), cos, cos, pad], axis=-1)
    sin128 = jnp.concatenate([jnp.zeros((t, MLA_NOPE), F32), sin, sin, pad], axis=-1)
    return cos128, sin128


def _ffn_rows(n):
    for cand in (512, 256, 128, 64, 32, 16, 8):
        if n % cand == 0:
            return cand
    raise ValueError(f"row count {n} is not a multiple of 8")


def kernel(x, meta, ln_ffn1, ffn1_wi, ffn1_wo, ln_mix, w_in, fox_bf, mla_gq, mla_wq, mla_gkv, mla_wkv, gdn_conv, gdn_alog, gdn_dtb, gdn_gon, lru_conv, lru_conv_b, lru_wa, lru_ba, lru_wx, lru_bx, lru_lam, w_gate, b_gate, w_branch, w_out, ln_ffn2, ffn2_wi, ffn2_wo, ln_final):
    b, s, d = x.shape
    t = BLOCK + s
    depth = w_in.shape[0]
    h = jnp.concatenate([jnp.zeros((b, PAD_LEN, d), x.dtype),
                         jnp.broadcast_to(meta.astype(x.dtype)[None], (b, N_META, d)), x], axis=1)
    tt = _row_tile(t)
    tm = _ffn_rows(b * t)
    tq = next(c for c in (512, 256, 128) if s % c == 0)
    cos128, sin128 = _rope_tables(t)
    tri = (jnp.arange(tt)[:, None] <= jnp.arange(tt)[None, :]).astype(BF16)
    gf = ln_final[None]
    for l in range(depth):
        lw = _layer_weights(l, w_in, fox_bf, mla_gq, mla_wq, mla_gkv, mla_wkv, gdn_conv, gdn_alog,
                            gdn_dtb, gdn_gon, lru_conv, lru_conv_b, lru_wa, lru_ba, lru_wx, lru_bx,
                            lru_lam, w_gate, b_gate, w_branch, w_out, ln_mix)
        h = _ffn(h.reshape(b * t, d), ln_ffn1[l][None], ffn1_wi[l].astype(BF16),
                 ffn1_wo[l].astype(BF16), gf, final_norm=False, tm=tm).reshape(b, t, d)
        (fq, fk, fv, fcum, mq, mk, mv, gqkv, gab, ggate, plru) = _inproj(
            h, lw, cos128, sin128, tri, tt=tt)
        y_fox = _attention(fq, fk, fv, fcum, head_slices=False, tq=tq)
        y_mla = _attention(mq, mk, mv, None, head_slices=True, tq=tq)
        y_gdn = _gdn(gqkv, gab, ggate, lw, tg=tt)
        y_lru = _lru(plru, lw, tl=tt)
        h = _merge(h, (y_fox, y_mla, y_gdn, y_lru), lw, tt=tt)
        h = _ffn(h.reshape(b * t, d), ln_ffn2[l][None], ffn2_wi[l].astype(BF16),
                 ffn2_wo[l].astype(BF16), gf, final_norm=(l == depth - 1), tm=tm).reshape(b, t, d)
    return h[:, BLOCK:]
```

```python
import functools
import math

import jax
import jax.numpy as jnp
from jax import lax
from jax.experimental import pallas as pl
from jax.experimental.pallas import tpu as pltpu

F32 = jnp.float32
BF16 = jnp.bfloat16

N_META = 16
BLOCK = 128
PAD_LEN = BLOCK - N_META
EPS = 1e-6
NEG_INF = -1e30
N_BRANCH = 4
D_BRANCH = 256
HEADS = 4
DH = 64
MLA_NOPE = 64
MLA_ROPE = 32
MLA_Q_RANK = 192
MLA_KV_RANK = 128
ROPE_BASE = 10000.0
GDN_CONV = 4
GDN_CHUNK = 64
LRU_CONV = 4
LRU_C = 8.0
HALO = 8
LOG2E = 1.4426950408889634

OFF_FOX_QKV = 0
OFF_FOX_F = OFF_FOX_QKV + 3 * HEADS * DH
OFF_MLA_CQ = OFF_FOX_F + HEADS
OFF_MLA_CKV = OFF_MLA_CQ + MLA_Q_RANK
OFF_MLA_KR = OFF_MLA_CKV + MLA_KV_RANK
OFF_GDN_QKV = OFF_MLA_KR + MLA_ROPE
OFF_GDN_A = OFF_GDN_QKV + HEADS * 3 * DH
OFF_GDN_B = OFF_GDN_A + HEADS
OFF_GDN_G = OFF_GDN_B + HEADS
OFF_LRU = OFF_GDN_G + HEADS * DH
N_IN = OFF_LRU + D_BRANCH

VMEM_LIMIT = 56 * 1024 * 1024


def _cparams(*sem):
    return pltpu.CompilerParams(dimension_semantics=sem, vmem_limit_bytes=VMEM_LIMIT)


def _resident(shape):
    nd = len(shape)
    return pl.BlockSpec(shape, lambda *_: (0,) * nd, pipeline_mode=pl.Buffered(1))


def _dot(a, b):
    return jnp.dot(a, b, preferred_element_type=F32)


def _dot_nt(a, b):
    return lax.dot_general(a, b, (((1,), (1,)), ((), ())), preferred_element_type=F32)


def _dot_tn(a, b):
    return lax.dot_general(a, b, (((0,), (0,)), ((), ())), preferred_element_type=F32)


def _split3(x):
    hi = x.astype(BF16)
    r1 = x - hi.astype(F32)
    mid = r1.astype(BF16)
    lo = (r1 - mid.astype(F32)).astype(BF16)
    return hi, mid, lo


def _dot_exact_rhs(x, m):
    hi, mid, lo = _split3(x)
    return _dot(hi, m) + _dot(mid, m) + _dot(lo, m)


def _dot_exact_lhs(m, x):
    hi, mid, lo = _split3(x)
    return _dot(m, hi) + _dot(m, mid) + _dot(m, lo)


def _rms(x, g):
    var = jnp.mean(x * x, axis=-1, keepdims=True)
    return x * lax.rsqrt(var + EPS) * g


def _sigmoid(x):
    return 1.0 / (1.0 + jnp.exp(-x))


def _softplus(x):
    return jnp.maximum(x, 0.0) + jnp.log1p(jnp.exp(-jnp.abs(x)))


def _iota(shape, dim):
    return lax.broadcasted_iota(jnp.int32, shape, dim)


def _row_tile(t):
    for cand in (384, 256, 512, 128):
        if t % cand == 0:
            return cand
    raise ValueError(f"T={t} has no supported row tile")


def _ffn_kernel(*refs, d_ff, final_norm):
    *x_refs, g_ref, wi_ref, wo_ref, gf_ref, o_ref = refs
    x = jnp.concatenate([r[...] for r in x_refs], axis=0) if len(x_refs) > 1 else x_refs[0][...]
    xn = _rms(x, g_ref[...]).astype(BF16)
    gu = _dot(xn, wi_ref[...])
    g = gu[:, :d_ff]
    u = gu[:, d_ff:]
    act = (g * _sigmoid(g) * u).astype(BF16)
    y = x + 0.5 * _dot(act, wo_ref[...])
    if final_norm:
        y = _rms(y, gf_ref[...])
    o_ref[...] = y


def _ffn(h2, g, wi, wo, gf, *, tm):
    n, d = h2.shape
    d_ff = wo.shape[0]
    return pl.pallas_call(
        functools.partial(_ffn_kernel, d_ff=d_ff, final_norm=False),
        grid=(n // tm,),
        in_specs=[pl.BlockSpec((tm, d), lambda i: (i, 0)),
                  _resident((1, d)), _resident(wi.shape), _resident(wo.shape), _resident((1, d))],
        out_specs=pl.BlockSpec((tm, d), lambda i: (i, 0)),
        out_shape=jax.ShapeDtypeStruct((n, d), F32),
        compiler_params=_cparams("parallel"),
        name="ffn",
    )(h2, g, wi, wo, gf)


def _ffn_out(h, g, wi, wo, gf, *, tf):
    b, t, d = h.shape
    d_ff = wo.shape[0]
    nblk = tf // BLOCK
    x_specs = [pl.BlockSpec((None, BLOCK, d), lambda i, j, k=k: (i, nblk * j + 1 + k, 0))
               for k in range(nblk)]
    return pl.pallas_call(
        functools.partial(_ffn_kernel, d_ff=d_ff, final_norm=True),
        grid=(b, (t - BLOCK) // tf),
        in_specs=x_specs + [_resident((1, d)), _resident(wi.shape), _resident(wo.shape),
                            _resident((1, d))],
        out_specs=pl.BlockSpec((None, tf, d), lambda i, j: (i, j, 0)),
        out_shape=jax.ShapeDtypeStruct((b, t - BLOCK, d), F32),
        compiler_params=_cparams("parallel", "parallel"),
        name="ffn_out",
    )(*([h] * nblk), g, wi, wo, gf)


def _inproj_kernel(h_ref, g_ref, wfox_ref, wft_ref, bf_ref, wmla_ref, gq_ref, wq_ref, wqs_ref,
                   gkv_ref, wkv_ref, cos_ref, sin_ref, wgdn_ref, wlru_ref, tri_ref,
                   cw_ref, alog_ref, dtb_ref, ctri_ref, bdones_ref,
                   fq_ref, fk_ref, fv_ref, fcum_ref, mq_ref, mk_ref, mv_ref,
                   gqkv_ref, gab_ref, ggate_ref, lru_ref, carry_ref, ext_ref, *, tt):
    t = pl.program_id(1)
    w = HEADS * DH

    @pl.when(t == 0)
    def _():
        carry_ref[...] = jnp.zeros_like(carry_ref)
        ext_ref[0:HALO, :] = jnp.zeros((HALO, 3 * w), F32)

    @pl.when(t > 0)
    def _():
        ext_ref[0:HALO, :] = ext_ref[tt:tt + HALO, :]

    rows = t * tt + _iota((tt, 1), 0)
    u = jnp.where(rows >= PAD_LEN, _rms(h_ref[...], g_ref[...]), 0.0).astype(BF16)

    pg = _dot(u, wgdn_ref[...])
    fox = _dot(u, wfox_ref[...])
    pm = _dot(u, wmla_ref[...])
    z = _dot_nt(wft_ref[...], u) + bf_ref[...]
    lru_ref[...] = _dot(u, wlru_ref[...])

    ext_ref[HALO:HALO + tt, :] = pg[:, :3 * w]
    x_ext = ext_ref[...]
    conv = cw_ref[GDN_CONV - 1:GDN_CONV, :] * x_ext[HALO:, :]
    for d in range(1, GDN_CONV):
        kk = GDN_CONV - 1 - d
        conv = conv + cw_ref[kk:kk + 1, :] * pltpu.roll(x_ext, d, 0)[HALO:, :]
    qkv = conv * _sigmoid(conv)

    fq_ref[...] = (fox[:, :w] * (DH ** -0.5 * LOG2E)).astype(BF16)
    fk_ref[...] = fox[:, w:2 * w].astype(BF16)
    fv_ref[...] = fox[:, 2 * w:].astype(BF16)

    log_f = jnp.minimum(z, 0.0) - jnp.log1p(jnp.exp(-jnp.abs(z)))
    cum = _dot_exact_rhs(log_f, tri_ref[...]) + carry_ref[...]
    fcum_ref[...] = cum * LOG2E
    carry_ref[...] = cum[:, tt - 1:tt]

    cos = cos_ref[...]
    sin = sin_ref[...]
    cq = pm[:, :256]
    cqn = (cq * lax.rsqrt(jnp.sum(cq * cq, axis=-1, keepdims=True) / MLA_Q_RANK + EPS)
           * gq_ref[...]).astype(BF16)
    qa = _dot(cqn, wq_ref[...])
    qb = _dot(cqn, wqs_ref[...])
    ckv = pm[:, 256:384]
    ckvn = (_rms(ckv, gkv_ref[...])).astype(BF16)
    kv = _dot(ckvn, wkv_ref[...])
    kr = pm[:, 384:512] * cos + pm[:, 512:640] * sin
    scale = (MLA_NOPE + MLA_ROPE) ** -0.5 * LOG2E
    for hd in range(HEADS):
        sl = slice(hd * 128, (hd + 1) * 128)
        mq_ref[:, sl] = ((qa[:, sl] * cos + qb[:, sl] * sin) * scale).astype(BF16)
        mk_ref[:, sl] = (kv[:, sl] + kr).astype(BF16)
    mv_ref[...] = kv[:, HEADS * 128:].astype(BF16)

    bd_ones = bdones_ref[...]

    def l2n(x):
        return x * lax.rsqrt(_dot_exact_rhs(x * x, bd_ones) + EPS)

    gqkv_ref[:, 0:w] = l2n(qkv[:, :w]) * (DH ** -0.5)
    gqkv_ref[:, w:2 * w] = l2n(qkv[:, w:2 * w])
    gqkv_ref[:, 2 * w:] = qkv[:, 2 * w:]
    log_decay = -jnp.exp(alog_ref[...]) * _softplus(pg[:, 3 * w:4 * w] + dtb_ref[...])
    gab_ref[:, 0:w] = _dot_exact_lhs(ctri_ref[...], log_decay)
    gab_ref[:, w:] = _sigmoid(pg[:, 4 * w:5 * w])
    gate = pg[:, 5 * w:]
    ggate_ref[...] = gate * _sigmoid(gate)


def _inproj(h, lw, cos128, sin128, tri, ctri, bd_ones, *, tt):
    b, t, d = h.shape
    w = HEADS * DH
    row = lambda c: pl.BlockSpec((None, tt, c), lambda i, j: (i, j, 0))
    ins = [h, lw["ln_mix"], lw["w_fox"], lw["w_ft"], lw["b_f"], lw["w_mla"], lw["g_q"], lw["w_q"],
           lw["w_qs"], lw["g_kv"], lw["w_kv"], cos128, sin128, lw["w_gdn"], lw["w_lru"], tri,
           lw["gdn_conv"], lw["gdn_alog"], lw["gdn_dtb"], ctri, bd_ones]
    in_specs = [row(d)] + [_resident(a.shape) for a in ins[1:11]]
    in_specs += [pl.BlockSpec((tt, 128), lambda i, j: (j, 0))] * 2
    in_specs += [_resident(a.shape) for a in ins[13:]]
    out_shape = [jax.ShapeDtypeStruct((b, t, w), BF16)] * 3
    out_shape += [jax.ShapeDtypeStruct((b, 8, t), F32)]
    out_shape += [jax.ShapeDtypeStruct((b, t, 512), BF16)] * 2 + [jax.ShapeDtypeStruct((b, t, w), BF16)]
    out_shape += [jax.ShapeDtypeStruct((b, t, 3 * w), F32), jax.ShapeDtypeStruct((b, t, 2 * w), F32),
                  jax.ShapeDtypeStruct((b, t, w), F32), jax.ShapeDtypeStruct((b, t, w), F32)]
    out_specs = [row(w)] * 3 + [pl.BlockSpec((None, 8, tt), lambda i, j: (i, 0, j))]
    out_specs += [row(512)] * 2 + [row(w), row(3 * w), row(2 * w), row(w), row(w)]
    return pl.pallas_call(
        functools.partial(_inproj_kernel, tt=tt),
        grid=(b, t // tt),
        in_specs=in_specs, out_specs=out_specs, out_shape=out_shape,
        scratch_shapes=[pltpu.VMEM((8, 1), F32), pltpu.VMEM((tt + HALO, 3 * w), F32)],
        compiler_params=_cparams("parallel", "arbitrary"),
        name="inproj",
    )(*ins)


def _attn_kernel(*refs, tq, head_slices, has_bias):
    if has_bias:
        q_ref, k_ref, v_ref, bias_ref, o_ref, vp_ref, m_ref, acc_ref = refs
    else:
        q_ref, k_ref, v_ref, o_ref, vp_ref, m_ref, acc_ref = refs
        bias_ref = None
    t, wq = q_ref.shape
    wv = v_ref.shape[-1]
    half = wv // 2
    lane_head = _iota((1, wv), 1) // DH
    v = v_ref[...]
    for hd in range(HEADS):
        vp_ref[hd] = jnp.where(lane_head == hd, v, jnp.ones_like(v))

    def run_tile(q0, rows, steps):
        q = q_ref[pl.ds(q0, rows), :]
        if head_slices:
            qh = [q[:, hd * 128:(hd + 1) * 128] for hd in range(HEADS)]
        else:
            q_head = _iota((1, wq), 1) // DH
            qh = [jnp.where(q_head == hd, q, jnp.zeros_like(q)) for hd in range(HEADS)]
        for hd in range(HEADS):
            m_ref[hd, 0:rows, :] = jnp.full((rows, 128), NEG_INF, F32)
            acc_ref[hd, 0:rows, :] = jnp.zeros((rows, wv), F32)

        def step(k0, cols, mask):
            scores = []
            for hd in range(HEADS):
                if head_slices:
                    kblk = k_ref[pl.ds(k0, cols), hd * 128:(hd + 1) * 128]
                else:
                    kblk = k_ref[pl.ds(k0, cols), :]
                scores.append(_dot_nt(qh[hd], kblk))
            probs = []
            for hd, s in enumerate(scores):
                if has_bias:
                    s = s - bias_ref[hd:hd + 1, pl.ds(k0, cols)]
                if mask is not None:
                    s = jnp.where(mask, s, NEG_INF)
                m_old = m_ref[hd, 0:rows, :]
                m_new = jnp.maximum(m_old, jnp.max(s, axis=-1, keepdims=True))
                m_ref[hd, 0:rows, :] = m_new
                alpha = jnp.exp2(m_old - m_new)
                p = jnp.exp2(s - jnp.concatenate([m_new] * (cols // 128), axis=1))
                probs.append((p.astype(BF16), alpha))
            for hd, (p, alpha) in enumerate(probs):
                pv = _dot(p, vp_ref[hd, pl.ds(k0, cols), :])
                acc_ref[hd, 0:rows, :] = (acc_ref[hd, 0:rows, :]
                                          * jnp.concatenate([alpha] * (wv // 128), axis=1) + pv)

        steps(step)
        out = jnp.zeros((rows, wv), F32)
        for hd in range(HEADS):
            acc = acc_ref[hd, 0:rows, :]
            row_sum = jnp.concatenate([acc[:, half:], acc[:, :half]], axis=1)
            out = jnp.where(lane_head == hd, acc / row_sum, out)
        o_ref[pl.ds(q0, rows), :] = out.astype(o_ref.dtype)

    meta_keys = _iota((1, BLOCK), 1) >= PAD_LEN
    causal0 = (_iota((BLOCK, BLOCK), 1) <= _iota((BLOCK, BLOCK), 0)) & meta_keys
    run_tile(0, BLOCK, lambda step: step(0, BLOCK, causal0))

    causal = _iota((tq, tq), 1) <= _iota((tq, tq), 0)

    def outer(i, carry):
        q0 = pl.multiple_of(BLOCK + i * tq, 128)

        def steps(step):
            step(0, BLOCK, meta_keys)

            def full(j, c):
                step(pl.multiple_of(BLOCK + j * tq, 128), tq, None)
                return c

            lax.fori_loop(0, i, full, 0)
            step(q0, tq, causal)

        run_tile(q0, tq, steps)
        return carry

    lax.fori_loop(0, (t - BLOCK) // tq, outer, 0)


def _attention(q, k, v, bias, *, head_slices, tq):
    b, t, wq = q.shape
    wk = k.shape[-1]
    wv = v.shape[-1]
    assert wv == 2 * 128 and (t - BLOCK) % tq == 0
    full = lambda c: pl.BlockSpec((None, t, c), lambda i: (i, 0, 0))
    ins = [q, k, v]
    in_specs = [full(wq), full(wk), full(wv)]
    if bias is not None:
        ins.append(bias)
        in_specs.append(pl.BlockSpec((None, 8, t), lambda i: (i, 0, 0)))
    return pl.pallas_call(
        functools.partial(_attn_kernel, tq=tq, head_slices=head_slices, has_bias=bias is not None),
        grid=(b,),
        in_specs=in_specs,
        out_specs=full(wv),
        out_shape=jax.ShapeDtypeStruct((b, t, wv), BF16),
        scratch_shapes=[pltpu.VMEM((HEADS, t, wv), BF16), pltpu.VMEM((HEADS, tq, 128), F32),
                        pltpu.VMEM((HEADS, tq, wv), F32)],
        compiler_params=_cparams("parallel"),
        name="attn_mla" if head_slices else "attn_fox",
    )(*ins)


def _gdn_kernel(qkv_ref, ab_ref, gate_ref, gon_ref, o_ref, o_s, state_ref, *, tg, nb,
                group_chunks):
    t = pl.program_id(1)
    c = GDN_CHUNK
    w = HEADS * DH

    @pl.when(t == 0)
    def _():
        state_ref[...] = jnp.zeros_like(state_ref)

    bd = (_iota((w, w), 0) // DH) == (_iota((w, w), 1) // DH)
    bd_ones = jnp.where(bd, 1.0, 0.0).astype(BF16)
    row = _iota((c, w), 0)
    col = _iota((c, w), 1) % DH
    eye_rep = row == col
    incl_rep = row >= col
    strict_rep = row > col
    tile4 = jnp.where(_iota((c, w), 0) == col, 1.0, 0.0).astype(BF16)

    def zmat(x):
        xb = x.astype(BF16)
        return jnp.where(bd, jnp.concatenate([xb] * HEADS, axis=0), jnp.zeros((w, w), BF16))

    def mm(x, z):
        return _dot(x.astype(BF16), z)

    def prep(where):
        each = lambda f, *xs: [f(*a) for a in zip(*xs)]
        qc = [qkv_ref[bi, pl.ds(r0, c), 0:w] for bi, r0 in where]
        kc = [qkv_ref[bi, pl.ds(r0, c), w:2 * w] for bi, r0 in where]
        bc = [ab_ref[bi, pl.ds(r0, c), w:2 * w] for bi, r0 in where]
        kb = each(lambda k, b: k * b, kc, bc)
        vb = each(lambda br, b: qkv_ref[br[0], pl.ds(br[1], c), 2 * w:3 * w] * b, where, bc)
        big_g = [ab_ref[bi, pl.ds(r0, c), 0:w] for bi, r0 in where]
        zk = [jnp.where(bd, _dot_tn(k.astype(BF16), tile4), 0.0).astype(BF16) for k in kc]
        stack = lambda x, y: jnp.concatenate([x, y], axis=0)
        kq = each(lambda k, q, z: mm(stack(k, q), z), kb, qc, zk)
        g_row = [jnp.sum(jnp.where(eye_rep, g, 0.0), axis=0, keepdims=True) for g in big_g]
        decay = each(lambda g, r: jnp.exp(jnp.where(incl_rep, g - r, NEG_INF)), big_g, g_row)
        qk = each(lambda x, d: jnp.where(incl_rep, x[c:] * d, 0.0), kq, decay)
        p = each(lambda x, d: jnp.where(strict_rep, -(x[:c] * d), 0.0), kq, decay)
        tinv = [jnp.where(eye_rep, 1.0, 0.0) for _ in p]
        n_sq = int(math.log2(c)) - 1
        for i in range(n_sq):
            both = each(lambda x, tv: mm(stack(x, tv), zmat(x)), p, tinv)
            tinv = each(lambda tv, r: tv + r[c:], tinv, both)
            p = [r[:c] for r in both]
        tinv = each(lambda tv, x: tv + mm(tv, zmat(x)), tinv, p)
        e_g = [jnp.exp(g) for g in big_g]
        wmat = each(lambda tv, k, e: mm(tv, zmat(k * e)), tinv, kb, e_g)
        umat = each(lambda tv, x: mm(tv, zmat(x)), tinv, vb)
        wq = each(lambda wm, q, e: jnp.concatenate([wm, q * e], axis=0).astype(BF16), wmat, qc, e_g)
        k_dec = each(lambda k, g: (k * jnp.exp(g[c - 1:c, :] - g)).astype(BF16), kc, big_g)
        g_last = [jnp.exp(g[c - 1:c, :]) for g in big_g]
        return list(zip(wq, umat, qk, k_dec, g_last))

    def group(gi, carry):
        base = gi * (group_chunks * c)
        r0s = [pl.multiple_of(base + j * c, c) for j in range(group_chunks)]
        pre = prep([(bi, r0) for r0 in r0s for bi in range(nb)])
        states = [state_ref[bi] for bi in range(nb)]
        for j, r0 in enumerate(r0s):
            for bi in range(nb):
                wq, umat, qk, k_dec, gl = pre[j * nb + bi]
                ws_qs = _dot(wq, states[bi].astype(BF16))
                v_new = umat - ws_qs[:c]
                o_s[bi, pl.ds(r0, c), :] = ws_qs[c:] + mm(qk, zmat(v_new))
                upd = _dot_tn(k_dec, v_new.astype(BF16))
                states[bi] = states[bi] * gl + jnp.where(bd, upd, 0.0)
        for bi in range(nb):
            state_ref[bi] = states[bi]
        return carry

    lax.fori_loop(0, tg // (group_chunks * c), group, 0)

    for bi in range(nb):
        o = o_s[bi]
        ms = _dot_exact_rhs(o * o, bd_ones) * (1.0 / DH)
        o_ref[bi] = (o * lax.rsqrt(ms + EPS) * gon_ref[...] * gate_ref[bi]).astype(o_ref.dtype)


def _gdn(gqkv, gab, ggate, lw, *, tg):
    b, t, _ = gqkv.shape
    w = HEADS * DH
    nb = 2 if b % 2 == 0 else 1
    row = lambda c: pl.BlockSpec((nb, tg, c), lambda i, j: (i, j, 0))
    return pl.pallas_call(
        functools.partial(_gdn_kernel, tg=tg, nb=nb, group_chunks=next(
            g for g in (6, 4, 3, 2, 1) if (tg // GDN_CHUNK) % g == 0)),
        grid=(b // nb, t // tg),
        in_specs=[row(3 * w), row(2 * w), row(w), _resident((1, w))],
        out_specs=row(w),
        out_shape=jax.ShapeDtypeStruct((b, t, w), BF16),
        scratch_shapes=[pltpu.VMEM((nb, tg, w), F32), pltpu.VMEM((nb, w, w), F32)],
        compiler_params=_cparams("parallel", "arbitrary"),
        name="gdn",
    )(gqkv, gab, ggate, lw["gdn_gon"])


def _lru_kernel(x_ref, cw_ref, cb_ref, wa_ref, ba_ref, wx_ref, bx_ref, lam_ref, o_ref,
                ext_ref, h_ref, *, tl):
    t = pl.program_id(1)
    w = D_BRANCH

    @pl.when(t == 0)
    def _():
        ext_ref[0:HALO, :] = jnp.zeros((HALO, w), F32)
        h_ref[...] = jnp.zeros_like(h_ref)

    @pl.when(t > 0)
    def _():
        ext_ref[0:HALO, :] = ext_ref[tl:tl + HALO, :]

    ext_ref[HALO:HALO + tl, :] = x_ref[...]
    xr = jnp.zeros((tl, w), F32) + cb_ref[...]
    for kk in range(LRU_CONV):
        off = HALO - (LRU_CONV - 1) + kk
        xr = xr + cw_ref[kk:kk + 1, :] * ext_ref[off:off + tl, :]
    rows = _iota((tl, 1), 0)
    xr = jnp.where(t * tl + rows >= PAD_LEN, xr, 0.0)
    xb = xr.astype(BF16)
    r = _sigmoid(_dot(xb, wa_ref[...]) + ba_ref[...])
    ig = _sigmoid(_dot(xb, wx_ref[...]) + bx_ref[...])
    log_a = -LRU_C * r * _softplus(-lam_ref[...])
    a = jnp.exp(log_a)
    bv = jnp.sqrt(-jnp.tanh(log_a) * (a * a + 1.0)) * ig * xr
    s = 1
    while s < tl:
        keep = rows >= s
        a_sh = jnp.where(keep, pltpu.roll(a, s, 0), 1.0)
        b_sh = jnp.where(keep, pltpu.roll(bv, s, 0), 0.0)
        bv = a * b_sh + bv
        a = a * a_sh
        s *= 2
    h = a * h_ref[...] + bv
    h_ref[...] = h[tl - 1:tl, :]
    o_ref[...] = h.astype(o_ref.dtype)


def _lru(x, lw, *, tl):
    b, t, w = x.shape
    row = pl.BlockSpec((None, tl, w), lambda i, j: (i, j, 0))
    return pl.pallas_call(
        functools.partial(_lru_kernel, tl=tl),
        grid=(b, t // tl),
        in_specs=[row, _resident((LRU_CONV, w)), _resident((1, w)), _resident((w, w)),
                  _resident((1, w)), _resident((w, w)), _resident((1, w)), _resident((1, w))],
        out_specs=row,
        out_shape=jax.ShapeDtypeStruct((b, t, w), BF16),
        scratch_shapes=[pltpu.VMEM((tl + HALO, w), F32), pltpu.VMEM((1, w), F32)],
        compiler_params=_cparams("parallel", "arbitrary"),
        name="lru",
    )(x, lw["lru_conv"], lw["lru_conv_b"], lw["lru_wa"], lw["lru_ba"], lw["lru_wx"],
      lw["lru_bx"], lw["lru_lam"])


def _merge_kernel(h_ref, y0_ref, y1_ref, y2_ref, y3_ref, g_ref, wg_ref, bg_ref, wb_ref, wo_ref,
                  o_ref, *, tt):
    t = pl.program_id(1)
    h = h_ref[...]
    rows = t * tt + _iota((tt, 1), 0)
    u = jnp.where(rows >= PAD_LEN, _rms(h, g_ref[...]), 0.0).astype(BF16)
    merged = None
    for n, y_ref in enumerate((y0_ref, y1_ref, y2_ref, y3_ref)):
        gate = _sigmoid(_dot(u, wg_ref[n]) + bg_ref[n])
        term = gate * _dot(y_ref[...], wb_ref[n])
        merged = term if merged is None else merged + term
    o_ref[...] = h + _dot(merged.astype(BF16), wo_ref[...])


def _merge(h, ys, lw, *, tt):
    b, t, d = h.shape
    row = lambda c: pl.BlockSpec((None, tt, c), lambda i, j: (i, j, 0))
    return pl.pallas_call(
        functools.partial(_merge_kernel, tt=tt),
        grid=(b, t // tt),
        in_specs=[row(d)] + [row(D_BRANCH)] * N_BRANCH + [
            _resident((1, d)), _resident(lw["w_gate"].shape), _resident(lw["b_gate"].shape),
            _resident(lw["w_branch"].shape), _resident((d, d))],
        out_specs=row(d),
        out_shape=jax.ShapeDtypeStruct((b, t, d), F32),
        compiler_params=_cparams("parallel", "parallel"),
        name="merge",
    )(h, *ys, lw["ln_mix"], lw["w_gate"], lw["b_gate"], lw["w_branch"], lw["w_out"])


def _rope_swap(wr):
    half = wr.shape[-1] // 2
    return jnp.concatenate([-wr[..., half:], wr[..., :half]], axis=-1)


def _head_blocks(parts, width=128):
    rows = parts[0].shape[0]
    x = jnp.concatenate(parts, axis=-1)
    x = jnp.pad(x, ((0, 0), (0, 0), (0, width - x.shape[-1])))
    return x.reshape(rows, HEADS * width)


def _layer_weights(l, w_in, fox_bf, mla_gq, mla_wq, mla_gkv, mla_wkv, gdn_conv, gdn_alog, gdn_dtb,
                   gdn_gon, lru_conv, lru_conv_b, lru_wa, lru_ba, lru_wx, lru_bx, lru_lam,
                   w_gate, b_gate, w_branch, w_out, ln_mix):
    d = w_in.shape[1]
    wi = w_in[l]
    w = HEADS * DH
    lw = {"ln_mix": ln_mix[l][None]}
    lw["w_fox"] = wi[:, OFF_FOX_QKV:OFF_FOX_F].astype(BF16)
    lw["w_ft"] = jnp.pad(wi[:, OFF_FOX_F:OFF_MLA_CQ].T, ((0, 8 - HEADS), (0, 0))).astype(BF16)
    lw["b_f"] = jnp.pad(fox_bf[l], (0, 8 - HEADS))[:, None]
    z64 = jnp.zeros((d, 64), F32)
    z32 = jnp.zeros((d, 32), F32)
    kr = wi[:, OFF_MLA_KR:OFF_GDN_QKV]
    lw["w_mla"] = jnp.concatenate(
        [wi[:, OFF_MLA_CQ:OFF_MLA_CKV], z64, wi[:, OFF_MLA_CKV:OFF_MLA_KR],
         z64, kr, z32, z64, _rope_swap(kr), z32], axis=-1).astype(BF16)
    lw["g_q"] = jnp.pad(mla_gq[l], (0, 256 - MLA_Q_RANK))[None]
    wq = mla_wq[l].reshape(MLA_Q_RANK, HEADS, MLA_NOPE + MLA_ROPE)
    wq_nope, wq_rope = wq[..., :MLA_NOPE], wq[..., MLA_NOPE:]
    padq = lambda x: jnp.pad(x, ((0, 256 - MLA_Q_RANK), (0, 0))).astype(BF16)
    lw["w_q"] = padq(_head_blocks([wq_nope, wq_rope]))
    lw["w_qs"] = padq(_head_blocks([jnp.zeros_like(wq_nope), _rope_swap(wq_rope)]))
    lw["g_kv"] = mla_gkv[l][None]
    wkv = mla_wkv[l].reshape(MLA_KV_RANK, HEADS, MLA_NOPE + DH)
    lw["w_kv"] = jnp.concatenate(
        [_head_blocks([wkv[..., :MLA_NOPE]]), wkv[..., MLA_NOPE:].reshape(MLA_KV_RANK, w)],
        axis=-1).astype(BF16)
    rep = lambda x: jnp.repeat(x, DH, axis=-1)
    lw["w_gdn"] = jnp.concatenate(
        [wi[:, OFF_GDN_QKV:OFF_GDN_A], rep(wi[:, OFF_GDN_A:OFF_GDN_B]),
         rep(wi[:, OFF_GDN_B:OFF_GDN_G]), wi[:, OFF_GDN_G:OFF_LRU]], axis=-1).astype(BF16)
    lw["gdn_conv"] = gdn_conv[l]
    lw["gdn_alog"] = rep(gdn_alog[l])[None]
    lw["gdn_dtb"] = rep(gdn_dtb[l])[None]
    lw["gdn_gon"] = jnp.tile(gdn_gon[l], HEADS)[None]
    lw["w_lru"] = wi[:, OFF_LRU:N_IN].astype(BF16)
    lw["lru_conv"] = lru_conv[l]
    lw["lru_conv_b"] = lru_conv_b[l][None]
    lw["lru_wa"] = jax.scipy.linalg.block_diag(*lru_wa[l]).astype(BF16)
    lw["lru_ba"] = lru_ba[l][None]
    lw["lru_wx"] = jax.scipy.linalg.block_diag(*lru_wx[l]).astype(BF16)
    lw["lru_bx"] = lru_bx[l][None]
    lw["lru_lam"] = lru_lam[l][None]
    lw["w_gate"] = w_gate[l].astype(BF16)
    lw["b_gate"] = b_gate[l][:, None, :]
    lw["w_branch"] = w_branch[l].astype(BF16)
    lw["w_out"] = w_out[l].astype(BF16)
    return lw


def _rope_tables(t):
    rel = (jnp.arange(t) - PAD_LEN).astype(F32)
    inv_freq = ROPE_BASE ** (-(jnp.arange(0, MLA_ROPE, 2, dtype=F32) / MLA_ROPE))
    ang = rel[:, None] * inv_freq[None, :]
    cos, sin = jnp.cos(ang), jnp.sin(ang)
    pad = jnp.zeros((t, 128 - MLA_NOPE - MLA_ROPE), F32)
    cos128 = jnp.concatenate([jnp.ones((t, MLA_NOPE), F32), cos, cos, pad], axis=-1)
    sin128 = jnp.concatenate([jnp.zeros((t, MLA_NOPE), F32), sin, sin, pad], axis=-1)
    return cos128, sin128


def _ffn_rows(n):
    for cand in (512, 256, 128, 64, 32, 16, 8):
        if n % cand == 0:
            return cand
    raise ValueError(f"row count {n} is not a multiple of 8")


def kernel(x, meta, ln_ffn1, ffn1_wi, ffn1_wo, ln_mix, w_in, fox_bf, mla_gq, mla_wq, mla_gkv, mla_wkv, gdn_conv, gdn_alog, gdn_dtb, gdn_gon, lru_conv, lru_conv_b, lru_wa, lru_ba, lru_wx, lru_bx, lru_lam, w_gate, b_gate, w_branch, w_out, ln_ffn2, ffn2_wi, ffn2_wo, ln_final):
    b, s, d = x.shape
    t = BLOCK + s
    depth = w_in.shape[0]
    h = jnp.concatenate([jnp.zeros((b, PAD_LEN, d), x.dtype),
                         jnp.broadcast_to(meta.astype(x.dtype)[None], (b, N_META, d)), x], axis=1)
    tt = _row_tile(t)
    tm = _ffn_rows(b * t)
    tq = next(c for c in (512, 256, 128) if s % c == 0)
    cos128, sin128 = _rope_tables(t)
    idx = jnp.arange(tt)
    tri = (idx[:, None] <= idx[None, :]).astype(BF16)
    ctri = ((idx[:, None] >= idx[None, :])
            & (idx[:, None] // GDN_CHUNK == idx[None, :] // GDN_CHUNK)).astype(BF16)
    lane_head = jnp.arange(HEADS * DH) // DH
    bd_ones = (lane_head[:, None] == lane_head[None, :]).astype(BF16)
    gf = ln_final[None]
    for l in range(depth):
        lw = _layer_weights(l, w_in, fox_bf, mla_gq, mla_wq, mla_gkv, mla_wkv, gdn_conv, gdn_alog,
                            gdn_dtb, gdn_gon, lru_conv, lru_conv_b, lru_wa, lru_ba, lru_wx, lru_bx,
                            lru_lam, w_gate, b_gate, w_branch, w_out, ln_mix)
        h = _ffn(h.reshape(b * t, d), ln_ffn1[l][None], ffn1_wi[l].astype(BF16),
                 ffn1_wo[l].astype(BF16), gf, tm=tm).reshape(b, t, d)
        (fq, fk, fv, fcum, mq, mk, mv, gqkv, gab, ggate, plru) = _inproj(
            h, lw, cos128, sin128, tri, ctri, bd_ones, tt=tt)
        y_fox = _attention(fq, fk, fv, fcum, head_slices=False, tq=tq)
        y_mla = _attention(mq, mk, mv, None, head_slices=True, tq=tq)
        y_gdn = _gdn(gqkv, gab, ggate, lw, tg=tt)
        y_lru = _lru(plru, lw, tl=tt)
        h = _merge(h, (y_fox, y_mla, y_gdn, y_lru), lw, tt=tt)
        wi2, wo2 = ffn2_wi[l].astype(BF16), ffn2_wo[l].astype(BF16)
        if l == depth - 1:
            return _ffn_out(h, ln_ffn2[l][None], wi2, wo2, gf, tf=tq)
        h = _ffn(h.reshape(b * t, d), ln_ffn2[l][None], wi2, wo2, gf, tm=tm).reshape(b, t, d)
```

```python
import functools
import math

import jax
import jax.numpy as jnp
from jax import lax
from jax.experimental import pallas as pl
from jax.experimental.pallas import tpu as pltpu

F32 = jnp.float32
BF16 = jnp.bfloat16

N_META = 16
BLOCK = 128
PAD_LEN = BLOCK - N_META
EPS = 1e-6
NEG_INF = -1e30
N_BRANCH = 4
D_BRANCH = 256
HEADS = 4
DH = 64
MLA_NOPE = 64
MLA_ROPE = 32
MLA_Q_RANK = 192
MLA_KV_RANK = 128
ROPE_BASE = 10000.0
GDN_CONV = 4
GDN_CHUNK = 64
LRU_CONV = 4
LRU_C = 8.0
HALO = 8
LOG2E = 1.4426950408889634

OFF_FOX_QKV = 0
OFF_FOX_F = OFF_FOX_QKV + 3 * HEADS * DH
OFF_MLA_CQ = OFF_FOX_F + HEADS
OFF_MLA_CKV = OFF_MLA_CQ + MLA_Q_RANK
OFF_MLA_KR = OFF_MLA_CKV + MLA_KV_RANK
OFF_GDN_QKV = OFF_MLA_KR + MLA_ROPE
OFF_GDN_A = OFF_GDN_QKV + HEADS * 3 * DH
OFF_GDN_B = OFF_GDN_A + HEADS
OFF_GDN_G = OFF_GDN_B + HEADS
OFF_LRU = OFF_GDN_G + HEADS * DH
N_IN = OFF_LRU + D_BRANCH

VMEM_LIMIT = 56 * 1024 * 1024


def _cparams(*sem):
    return pltpu.CompilerParams(dimension_semantics=sem, vmem_limit_bytes=VMEM_LIMIT)


def _resident(shape):
    nd = len(shape)
    return pl.BlockSpec(shape, lambda *_: (0,) * nd, pipeline_mode=pl.Buffered(1))


def _dot(a, b):
    return jnp.dot(a, b, preferred_element_type=F32)


def _dot_nt(a, b):
    return lax.dot_general(a, b, (((1,), (1,)), ((), ())), preferred_element_type=F32)


def _dot_tn(a, b):
    return lax.dot_general(a, b, (((0,), (0,)), ((), ())), preferred_element_type=F32)


def _split3(x):
    hi = x.astype(BF16)
    r1 = x - hi.astype(F32)
    mid = r1.astype(BF16)
    lo = (r1 - mid.astype(F32)).astype(BF16)
    return hi, mid, lo


def _dot_exact_rhs(x, m):
    hi, mid, lo = _split3(x)
    return _dot(hi, m) + _dot(mid, m) + _dot(lo, m)


def _dot_exact_lhs(m, x):
    hi, mid, lo = _split3(x)
    return _dot(m, hi) + _dot(m, mid) + _dot(m, lo)


def _rms(x, g):
    var = jnp.mean(x * x, axis=-1, keepdims=True)
    return x * lax.rsqrt(var + EPS) * g


def _sigmoid(x):
    return 1.0 / (1.0 + jnp.exp(-x))


def _softplus(x):
    return jnp.maximum(x, 0.0) + jnp.log1p(jnp.exp(-jnp.abs(x)))


def _iota(shape, dim):
    return lax.broadcasted_iota(jnp.int32, shape, dim)


def _row_tile(t):
    for cand in (384, 256, 512, 128):
        if t % cand == 0:
            return cand
    raise ValueError(f"T={t} has no supported row tile")


def _ffn_kernel(*refs, d_ff, final_norm):
    *x_refs, g_ref, wi_ref, wo_ref, gf_ref, o_ref = refs
    x = jnp.concatenate([r[...] for r in x_refs], axis=0) if len(x_refs) > 1 else x_refs[0][...]
    xn = _rms(x, g_ref[...]).astype(BF16)
    gu = _dot(xn, wi_ref[...])
    g = gu[:, :d_ff]
    u = gu[:, d_ff:]
    act = (g * _sigmoid(g) * u).astype(BF16)
    y = x + 0.5 * _dot(act, wo_ref[...])
    if final_norm:
        y = _rms(y, gf_ref[...])
    o_ref[...] = y


def _ffn(h2, g, wi, wo, gf, *, tm):
    n, d = h2.shape
    d_ff = wo.shape[0]
    return pl.pallas_call(
        functools.partial(_ffn_kernel, d_ff=d_ff, final_norm=False),
        grid=(n // tm,),
        in_specs=[pl.BlockSpec((tm, d), lambda i: (i, 0)),
                  _resident((1, d)), _resident(wi.shape), _resident(wo.shape), _resident((1, d))],
        out_specs=pl.BlockSpec((tm, d), lambda i: (i, 0)),
        out_shape=jax.ShapeDtypeStruct((n, d), F32),
        compiler_params=_cparams("parallel"),
        name="ffn",
    )(h2, g, wi, wo, gf)


def _ffn_out(h, g, wi, wo, gf, *, tf):
    b, t, d = h.shape
    d_ff = wo.shape[0]
    nblk = tf // BLOCK
    x_specs = [pl.BlockSpec((None, BLOCK, d), lambda i, j, k=k: (i, nblk * j + 1 + k, 0))
               for k in range(nblk)]
    return pl.pallas_call(
        functools.partial(_ffn_kernel, d_ff=d_ff, final_norm=True),
        grid=(b, (t - BLOCK) // tf),
        in_specs=x_specs + [_resident((1, d)), _resident(wi.shape), _resident(wo.shape),
                            _resident((1, d))],
        out_specs=pl.BlockSpec((None, tf, d), lambda i, j: (i, j, 0)),
        out_shape=jax.ShapeDtypeStruct((b, t - BLOCK, d), F32),
        compiler_params=_cparams("parallel", "parallel"),
        name="ffn_out",
    )(*([h] * nblk), g, wi, wo, gf)


def _inproj_kernel(h_ref, g_ref, wfox_ref, wft_ref, bf_ref, wmla_ref, gq_ref, wq_ref, wqs_ref,
                   gkv_ref, wkv_ref, cos_ref, sin_ref, wgdn_ref, wlru_ref,
                   cw_ref, alog_ref, dtb_ref, ctri_ref, bdones_ref,
                   lcw_ref, lcb_ref, lwa_ref, lba_ref, lwx_ref, lbx_ref, lam_ref,
                   fq_ref, fk_ref, fv_ref, fcum_ref, mq_ref, mk_ref, mv_ref,
                   gqkv_ref, gab_ref, ggate_ref, lru_ref, carry_ref, ext_ref, lext_ref, lh_ref,
                   *, tt):
    t = pl.program_id(1)
    w = HEADS * DH

    @pl.when(t == 0)
    def _():
        carry_ref[...] = jnp.zeros_like(carry_ref)
        ext_ref[0:HALO, :] = jnp.zeros((HALO, 3 * w), F32)
        lext_ref[0:HALO, :] = jnp.zeros((HALO, D_BRANCH), F32)
        lh_ref[...] = jnp.zeros_like(lh_ref)

    @pl.when(t > 0)
    def _():
        ext_ref[0:HALO, :] = ext_ref[tt:tt + HALO, :]
        lext_ref[0:HALO, :] = lext_ref[tt:tt + HALO, :]

    rows = t * tt + _iota((tt, 1), 0)
    valid = rows >= PAD_LEN
    u = jnp.where(valid, _rms(h_ref[...], g_ref[...]), 0.0).astype(BF16)

    pg = _dot(u, wgdn_ref[...])
    fox = _dot(u, wfox_ref[...])
    pm = _dot(u, wmla_ref[...])
    z = _dot_nt(wft_ref[...], u) + bf_ref[...]
    lext_ref[HALO:HALO + tt, :] = _dot(u, wlru_ref[...])

    ext_ref[HALO:HALO + tt, :] = pg[:, :3 * w]
    x_ext = ext_ref[...]
    conv = cw_ref[GDN_CONV - 1:GDN_CONV, :] * x_ext[HALO:, :]
    for d in range(1, GDN_CONV):
        kk = GDN_CONV - 1 - d
        conv = conv + cw_ref[kk:kk + 1, :] * pltpu.roll(x_ext, d, 0)[HALO:, :]
    qkv = conv * _sigmoid(conv)

    fq_ref[...] = (fox[:, :w] * (DH ** -0.5 * LOG2E)).astype(BF16)
    fk_ref[...] = fox[:, w:2 * w].astype(BF16)
    fv_ref[...] = fox[:, 2 * w:].astype(BF16)

    log_f = jnp.minimum(z, 0.0) - jnp.log1p(jnp.exp(-jnp.abs(z)))
    lane = _iota((8, 128), 1)
    running = carry_ref[...]
    for blk in range(tt // 128):
        x = log_f[:, blk * 128:(blk + 1) * 128]
        s = 1
        while s < 128:
            x = x + jnp.where(lane >= s, pltpu.roll(x, s, 1), 0.0)
            s *= 2
        x = x + running
        fcum_ref[:, blk * 128:(blk + 1) * 128] = x * LOG2E
        running = x[:, 127:128]
    carry_ref[...] = running

    cos = cos_ref[...]
    sin = sin_ref[...]
    cq = pm[:, :256]
    cqn = (cq * lax.rsqrt(jnp.sum(cq * cq, axis=-1, keepdims=True) / MLA_Q_RANK + EPS)
           * gq_ref[...]).astype(BF16)
    qa = _dot(cqn, wq_ref[...])
    qb = _dot(cqn, wqs_ref[...])
    ckv = pm[:, 256:384]
    ckvn = (_rms(ckv, gkv_ref[...])).astype(BF16)
    kv = _dot(ckvn, wkv_ref[...])
    kr = pm[:, 384:512] * cos + pm[:, 512:640] * sin
    scale = (MLA_NOPE + MLA_ROPE) ** -0.5 * LOG2E
    for hd in range(HEADS):
        sl = slice(hd * 128, (hd + 1) * 128)
        mq_ref[:, sl] = ((qa[:, sl] * cos + qb[:, sl] * sin) * scale).astype(BF16)
        mk_ref[:, sl] = (kv[:, sl] + kr).astype(BF16)
    mv_ref[...] = kv[:, HEADS * 128:].astype(BF16)

    bd_ones = bdones_ref[...]

    def l2n(x):
        return x * lax.rsqrt(_dot_exact_rhs(x * x, bd_ones) + EPS)

    gqkv_ref[:, 0:w] = l2n(qkv[:, :w]) * (DH ** -0.5)
    gqkv_ref[:, w:2 * w] = l2n(qkv[:, w:2 * w])
    gqkv_ref[:, 2 * w:] = qkv[:, 2 * w:]
    log_decay = -jnp.exp(alog_ref[...]) * _softplus(pg[:, 3 * w:4 * w] + dtb_ref[...])
    gab_ref[:, 0:w] = _dot_exact_lhs(ctri_ref[...], log_decay)
    gab_ref[:, w:] = _sigmoid(pg[:, 4 * w:5 * w])
    gate = pg[:, 5 * w:]
    ggate_ref[...] = gate * _sigmoid(gate)

    l_ext = lext_ref[...]
    xr = lcb_ref[...] + lcw_ref[LRU_CONV - 1:LRU_CONV, :] * l_ext[HALO:, :]
    for d in range(1, LRU_CONV):
        kk = LRU_CONV - 1 - d
        xr = xr + lcw_ref[kk:kk + 1, :] * pltpu.roll(l_ext, d, 0)[HALO:, :]
    xr = jnp.where(valid, xr, 0.0)
    xb = xr.astype(BF16)
    r = _sigmoid(_dot(xb, lwa_ref[...]) + lba_ref[...])
    ig = _sigmoid(_dot(xb, lwx_ref[...]) + lbx_ref[...])
    log_a = -LRU_C * r * _softplus(-lam_ref[...])
    a = jnp.exp(log_a)
    bv = jnp.sqrt(-jnp.tanh(log_a) * (a * a + 1.0)) * ig * xr
    local = _iota((tt, 1), 0)
    s = 1
    while s < tt:
        keep = local >= s
        a_sh = jnp.where(keep, pltpu.roll(a, s, 0), 1.0)
        b_sh = jnp.where(keep, pltpu.roll(bv, s, 0), 0.0)
        bv = a * b_sh + bv
        a = a * a_sh
        s *= 2
    hs = a * lh_ref[...] + bv
    lh_ref[...] = hs[tt - 1:tt, :]
    lru_ref[...] = hs.astype(lru_ref.dtype)


def _inproj(h, lw, cos128, sin128, ctri, bd_ones, *, tt):
    b, t, d = h.shape
    w = HEADS * DH
    row = lambda c: pl.BlockSpec((None, tt, c), lambda i, j: (i, j, 0))
    ins = [h, lw["ln_mix"], lw["w_fox"], lw["w_ft"], lw["b_f"], lw["w_mla"], lw["g_q"], lw["w_q"],
           lw["w_qs"], lw["g_kv"], lw["w_kv"], cos128, sin128, lw["w_gdn"], lw["w_lru"],
           lw["gdn_conv"], lw["gdn_alog"], lw["gdn_dtb"], ctri, bd_ones,
           lw["lru_conv"], lw["lru_conv_b"], lw["lru_wa"], lw["lru_ba"], lw["lru_wx"], lw["lru_bx"],
           lw["lru_lam"]]
    in_specs = [row(d)] + [_resident(a.shape) for a in ins[1:11]]
    in_specs += [pl.BlockSpec((tt, 128), lambda i, j: (j, 0))] * 2
    in_specs += [_resident(a.shape) for a in ins[13:]]
    out_shape = [jax.ShapeDtypeStruct((b, t, w), BF16)] * 3
    out_shape += [jax.ShapeDtypeStruct((b, 8, t), F32)]
    out_shape += [jax.ShapeDtypeStruct((b, t, 512), BF16)] * 2 + [jax.ShapeDtypeStruct((b, t, w), BF16)]
    out_shape += [jax.ShapeDtypeStruct((b, t, 3 * w), F32), jax.ShapeDtypeStruct((b, t, 2 * w), F32),
                  jax.ShapeDtypeStruct((b, t, w), F32), jax.ShapeDtypeStruct((b, t, D_BRANCH), BF16)]
    out_specs = [row(w)] * 3 + [pl.BlockSpec((None, 8, tt), lambda i, j: (i, 0, j))]
    out_specs += [row(512)] * 2 + [row(w), row(3 * w), row(2 * w), row(w), row(w)]
    return pl.pallas_call(
        functools.partial(_inproj_kernel, tt=tt),
        grid=(b, t // tt),
        in_specs=in_specs, out_specs=out_specs, out_shape=out_shape,
        scratch_shapes=[pltpu.VMEM((8, 1), F32), pltpu.VMEM((tt + HALO, 3 * w), F32),
                        pltpu.VMEM((tt + HALO, D_BRANCH), F32), pltpu.VMEM((1, D_BRANCH), F32)],
        compiler_params=_cparams("parallel", "arbitrary"),
        name="inproj",
    )(*ins)


def _attn_kernel(*refs, tq, head_slices, has_bias):
    if has_bias:
        q_ref, k_ref, v_ref, bias_ref, o_ref, vp_ref, m_ref, acc_ref = refs
    else:
        q_ref, k_ref, v_ref, o_ref, vp_ref, m_ref, acc_ref = refs
        bias_ref = None
    t, wq = q_ref.shape
    wv = v_ref.shape[-1]
    half = wv // 2
    lane_head = _iota((1, wv), 1) // DH
    v = v_ref[...]
    for hd in range(HEADS):
        vp_ref[hd] = jnp.where(lane_head == hd, v, jnp.ones_like(v))

    def run_tile(q0, rows, steps):
        q = q_ref[pl.ds(q0, rows), :]
        if head_slices:
            qh = [q[:, hd * 128:(hd + 1) * 128] for hd in range(HEADS)]
        else:
            q_head = _iota((1, wq), 1) // DH
            qh = [jnp.where(q_head == hd, q, jnp.zeros_like(q)) for hd in range(HEADS)]
        for hd in range(HEADS):
            m_ref[hd, 0:rows, :] = jnp.full((rows, 128), NEG_INF, F32)
            acc_ref[hd, 0:rows, :] = jnp.zeros((rows, wv), F32)

        def scores(k0, cols):
            out = []
            for hd in range(HEADS):
                if head_slices:
                    kblk = k_ref[pl.ds(k0, cols), hd * 128:(hd + 1) * 128]
                else:
                    kblk = k_ref[pl.ds(k0, cols), :]
                out.append(_dot_nt(qh[hd], kblk))
            return out

        def consume(sc, k0, cols, mask):
            probs = []
            for hd, s in enumerate(sc):
                if has_bias:
                    s = s - bias_ref[hd:hd + 1, pl.ds(k0, cols)]
                if mask is not None:
                    s = jnp.where(mask, s, NEG_INF)
                m_old = m_ref[hd, 0:rows, :]
                m_new = jnp.maximum(m_old, jnp.max(s, axis=-1, keepdims=True))
                m_ref[hd, 0:rows, :] = m_new
                alpha = jnp.exp2(m_old - m_new)
                p = jnp.exp2(s - jnp.concatenate([m_new] * (cols // 128), axis=1))
                probs.append((p.astype(BF16), alpha))
            for hd, (p, alpha) in enumerate(probs):
                pv = _dot(p, vp_ref[hd, pl.ds(k0, cols), :])
                acc_ref[hd, 0:rows, :] = (acc_ref[hd, 0:rows, :]
                                          * jnp.concatenate([alpha] * (wv // 128), axis=1) + pv)

        steps(scores, consume)
        out = jnp.zeros((rows, wv), F32)
        for hd in range(HEADS):
            acc = acc_ref[hd, 0:rows, :]
            row_sum = jnp.concatenate([acc[:, half:], acc[:, :half]], axis=1)
            out = jnp.where(lane_head == hd, acc / row_sum, out)
        o_ref[pl.ds(q0, rows), :] = out.astype(o_ref.dtype)

    meta_keys = _iota((1, BLOCK), 1) >= PAD_LEN
    causal0 = (_iota((BLOCK, BLOCK), 1) <= _iota((BLOCK, BLOCK), 0)) & meta_keys
    run_tile(0, BLOCK, lambda scores, consume: consume(scores(0, BLOCK), 0, BLOCK, causal0))

    causal = _iota((tq, tq), 1) <= _iota((tq, tq), 0)

    def outer(i, carry):
        q0 = pl.multiple_of(BLOCK + i * tq, 128)
        block = lambda j: pl.multiple_of(BLOCK + j * tq, 128)

        def steps(scores, consume):
            consume(scores(0, BLOCK), 0, BLOCK, meta_keys)

            def full(j, c):
                consume(scores(block(j), tq), block(j), tq, None)
                return c

            lax.fori_loop(0, i, full, 0)
            consume(scores(q0, tq), q0, tq, causal)

        run_tile(q0, tq, steps)
        return carry

    lax.fori_loop(0, (t - BLOCK) // tq, outer, 0)


def _attention(q, k, v, bias, *, head_slices, tq):
    b, t, wq = q.shape
    wk = k.shape[-1]
    wv = v.shape[-1]
    assert wv == 2 * 128 and (t - BLOCK) % tq == 0
    full = lambda c: pl.BlockSpec((None, t, c), lambda i: (i, 0, 0))
    ins = [q, k, v]
    in_specs = [full(wq), full(wk), full(wv)]
    if bias is not None:
        ins.append(bias)
        in_specs.append(pl.BlockSpec((None, 8, t), lambda i: (i, 0, 0)))
    return pl.pallas_call(
        functools.partial(_attn_kernel, tq=tq, head_slices=head_slices, has_bias=bias is not None),
        grid=(b,),
        in_specs=in_specs,
        out_specs=full(wv),
        out_shape=jax.ShapeDtypeStruct((b, t, wv), BF16),
        scratch_shapes=[pltpu.VMEM((HEADS, t, wv), BF16), pltpu.VMEM((HEADS, tq, 128), F32),
                        pltpu.VMEM((HEADS, tq, wv), F32)],
        compiler_params=_cparams("parallel"),
        name="attn_mla" if head_slices else "attn_fox",
    )(*ins)


def _gdn_kernel(qkv_ref, ab_ref, gate_ref, gon_ref, o_ref, o_s, state_ref, *, tg, nb,
                group_chunks):
    t = pl.program_id(1)
    c = GDN_CHUNK
    w = HEADS * DH

    @pl.when(t == 0)
    def _():
        state_ref[...] = jnp.zeros_like(state_ref)

    bd = (_iota((w, w), 0) // DH) == (_iota((w, w), 1) // DH)
    bd_ones = jnp.where(bd, 1.0, 0.0).astype(BF16)
    row = _iota((c, w), 0)
    col = _iota((c, w), 1) % DH
    eye_rep = row == col
    incl_rep = row >= col
    strict_rep = row > col
    tile4 = jnp.where(_iota((c, w), 0) == col, 1.0, 0.0).astype(BF16)

    def zmat(x):
        xb = x.astype(BF16)
        return jnp.where(bd, jnp.concatenate([xb] * HEADS, axis=0), jnp.zeros((w, w), BF16))

    def mm(x, z):
        return _dot(x.astype(BF16), z)

    def prep(where):
        each = lambda f, *xs: [f(*a) for a in zip(*xs)]
        qc = [qkv_ref[bi, pl.ds(r0, c), 0:w] for bi, r0 in where]
        kc = [qkv_ref[bi, pl.ds(r0, c), w:2 * w] for bi, r0 in where]
        bc = [ab_ref[bi, pl.ds(r0, c), w:2 * w] for bi, r0 in where]
        kb = each(lambda k, b: k * b, kc, bc)
        vb = each(lambda br, b: qkv_ref[br[0], pl.ds(br[1], c), 2 * w:3 * w] * b, where, bc)
        big_g = [ab_ref[bi, pl.ds(r0, c), 0:w] for bi, r0 in where]
        zk = [jnp.where(bd, _dot_tn(k.astype(BF16), tile4), 0.0).astype(BF16) for k in kc]
        stack = lambda x, y: jnp.concatenate([x, y], axis=0)
        kq = each(lambda k, q, z: mm(stack(k, q), z), kb, qc, zk)
        g_row = [jnp.sum(jnp.where(eye_rep, g, 0.0), axis=0, keepdims=True) for g in big_g]
        decay = each(lambda g, r: jnp.exp(jnp.where(incl_rep, g - r, NEG_INF)), big_g, g_row)
        qk = each(lambda x, d: jnp.where(incl_rep, x[c:] * d, 0.0), kq, decay)
        p = each(lambda x, d: jnp.where(strict_rep, -(x[:c] * d), 0.0), kq, decay)
        tinv = [jnp.where(eye_rep, 1.0, 0.0) for _ in p]
        n_sq = int(math.log2(c)) - 1
        for i in range(n_sq):
            both = each(lambda x, tv: mm(stack(x, tv), zmat(x)), p, tinv)
            tinv = each(lambda tv, r: tv + r[c:], tinv, both)
            p = [r[:c] for r in both]
        tinv = each(lambda tv, x: tv + mm(tv, zmat(x)), tinv, p)
        e_g = [jnp.exp(g) for g in big_g]
        wmat = each(lambda tv, k, e: mm(tv, zmat(k * e)), tinv, kb, e_g)
        umat = each(lambda tv, x: mm(tv, zmat(x)), tinv, vb)
        wq = each(lambda wm, q, e: jnp.concatenate([wm, q * e], axis=0).astype(BF16), wmat, qc, e_g)
        k_dec = each(lambda k, g: (k * jnp.exp(g[c - 1:c, :] - g)).astype(BF16), kc, big_g)
        g_last = [jnp.exp(g[c - 1:c, :]) for g in big_g]
        return list(zip(wq, umat, qk, k_dec, g_last))

    def group(gi, carry):
        base = gi * (group_chunks * c)
        r0s = [pl.multiple_of(base + j * c, c) for j in range(group_chunks)]
        pre = prep([(bi, r0) for r0 in r0s for bi in range(nb)])
        states = [state_ref[bi] for bi in range(nb)]
        for j, r0 in enumerate(r0s):
            for bi in range(nb):
                wq, umat, qk, k_dec, gl = pre[j * nb + bi]
                ws_qs = _dot(wq, states[bi].astype(BF16))
                v_new = umat - ws_qs[:c]
                o_s[bi, pl.ds(r0, c), :] = ws_qs[c:] + mm(qk, zmat(v_new))
                upd = _dot_tn(k_dec, v_new.astype(BF16))
                states[bi] = states[bi] * gl + jnp.where(bd, upd, 0.0)
        for bi in range(nb):
            state_ref[bi] = states[bi]
        return carry

    lax.fori_loop(0, tg // (group_chunks * c), group, 0)

    for bi in range(nb):
        o = o_s[bi]
        ms = _dot_exact_rhs(o * o, bd_ones) * (1.0 / DH)
        o_ref[bi] = (o * lax.rsqrt(ms + EPS) * gon_ref[...] * gate_ref[bi]).astype(o_ref.dtype)


def _gdn(gqkv, gab, ggate, lw, *, tg):
    b, t, _ = gqkv.shape
    w = HEADS * DH
    nb = 2 if b % 2 == 0 else 1
    row = lambda c: pl.BlockSpec((nb, tg, c), lambda i, j: (i, j, 0))
    return pl.pallas_call(
        functools.partial(_gdn_kernel, tg=tg, nb=nb, group_chunks=next(
            g for g in (6, 4, 3, 2, 1) if (tg // GDN_CHUNK) % g == 0)),
        grid=(b // nb, t // tg),
        in_specs=[row(3 * w), row(2 * w), row(w), _resident((1, w))],
        out_specs=row(w),
        out_shape=jax.ShapeDtypeStruct((b, t, w), BF16),
        scratch_shapes=[pltpu.VMEM((nb, tg, w), F32), pltpu.VMEM((nb, w, w), F32)],
        compiler_params=_cparams("parallel", "arbitrary"),
        name="gdn",
    )(gqkv, gab, ggate, lw["gdn_gon"])


def _merge_kernel(h_ref, y0_ref, y1_ref, y2_ref, y3_ref, g_ref, wg_ref, bg_ref, wb_ref, wo_ref,
                  o_ref, *, tt):
    t = pl.program_id(1)
    h = h_ref[...]
    rows = t * tt + _iota((tt, 1), 0)
    u = jnp.where(rows >= PAD_LEN, _rms(h, g_ref[...]), 0.0).astype(BF16)
    merged = None
    for n, y_ref in enumerate((y0_ref, y1_ref, y2_ref, y3_ref)):
        gate = _sigmoid(_dot(u, wg_ref[n]) + bg_ref[n])
        term = gate * _dot(y_ref[...], wb_ref[n])
        merged = term if merged is None else merged + term
    o_ref[...] = h + _dot(merged.astype(BF16), wo_ref[...])


def _merge(h, ys, lw, *, tt):
    b, t, d = h.shape
    row = lambda c: pl.BlockSpec((None, tt, c), lambda i, j: (i, j, 0))
    return pl.pallas_call(
        functools.partial(_merge_kernel, tt=tt),
        grid=(b, t // tt),
        in_specs=[row(d)] + [row(D_BRANCH)] * N_BRANCH + [
            _resident((1, d)), _resident(lw["w_gate"].shape), _resident(lw["b_gate"].shape),
            _resident(lw["w_branch"].shape), _resident((d, d))],
        out_specs=row(d),
        out_shape=jax.ShapeDtypeStruct((b, t, d), F32),
        compiler_params=_cparams("parallel", "parallel"),
        name="merge",
    )(h, *ys, lw["ln_mix"], lw["w_gate"], lw["b_gate"], lw["w_branch"], lw["w_out"])


def _rope_swap(wr):
    half = wr.shape[-1] // 2
    return jnp.concatenate([-wr[..., half:], wr[..., :half]], axis=-1)


def _head_blocks(parts, width=128):
    rows = parts[0].shape[0]
    x = jnp.concatenate(parts, axis=-1)
    x = jnp.pad(x, ((0, 0), (0, 0), (0, width - x.shape[-1])))
    return x.reshape(rows, HEADS * width)


def _layer_weights(l, w_in, fox_bf, mla_gq, mla_wq, mla_gkv, mla_wkv, gdn_conv, gdn_alog, gdn_dtb,
                   gdn_gon, lru_conv, lru_conv_b, lru_wa, lru_ba, lru_wx, lru_bx, lru_lam,
                   w_gate, b_gate, w_branch, w_out, ln_mix):
    d = w_in.shape[1]
    wi = w_in[l]
    w = HEADS * DH
    lw = {"ln_mix": ln_mix[l][None]}
    lw["w_fox"] = wi[:, OFF_FOX_QKV:OFF_FOX_F].astype(BF16)
    lw["w_ft"] = jnp.pad(wi[:, OFF_FOX_F:OFF_MLA_CQ].T, ((0, 8 - HEADS), (0, 0))).astype(BF16)
    lw["b_f"] = jnp.pad(fox_bf[l], (0, 8 - HEADS))[:, None]
    z64 = jnp.zeros((d, 64), F32)
    z32 = jnp.zeros((d, 32), F32)
    kr = wi[:, OFF_MLA_KR:OFF_GDN_QKV]
    lw["w_mla"] = jnp.concatenate(
        [wi[:, OFF_MLA_CQ:OFF_MLA_CKV], z64, wi[:, OFF_MLA_CKV:OFF_MLA_KR],
         z64, kr, z32, z64, _rope_swap(kr), z32], axis=-1).astype(BF16)
    lw["g_q"] = jnp.pad(mla_gq[l], (0, 256 - MLA_Q_RANK))[None]
    wq = mla_wq[l].reshape(MLA_Q_RANK, HEADS, MLA_NOPE + MLA_ROPE)
    wq_nope, wq_rope = wq[..., :MLA_NOPE], wq[..., MLA_NOPE:]
    padq = lambda x: jnp.pad(x, ((0, 256 - MLA_Q_RANK), (0, 0))).astype(BF16)
    lw["w_q"] = padq(_head_blocks([wq_nope, wq_rope]))
    lw["w_qs"] = padq(_head_blocks([jnp.zeros_like(wq_nope), _rope_swap(wq_rope)]))
    lw["g_kv"] = mla_gkv[l][None]
    wkv = mla_wkv[l].reshape(MLA_KV_RANK, HEADS, MLA_NOPE + DH)
    lw["w_kv"] = jnp.concatenate(
        [_head_blocks([wkv[..., :MLA_NOPE]]), wkv[..., MLA_NOPE:].reshape(MLA_KV_RANK, w)],
        axis=-1).astype(BF16)
    rep = lambda x: jnp.repeat(x, DH, axis=-1)
    lw["w_gdn"] = jnp.concatenate(
        [wi[:, OFF_GDN_QKV:OFF_GDN_A], rep(wi[:, OFF_GDN_A:OFF_GDN_B]),
         rep(wi[:, OFF_GDN_B:OFF_GDN_G]), wi[:, OFF_GDN_G:OFF_LRU]], axis=-1).astype(BF16)
    lw["gdn_conv"] = gdn_conv[l]
    lw["gdn_alog"] = rep(gdn_alog[l])[None]
    lw["gdn_dtb"] = rep(gdn_dtb[l])[None]
    lw["gdn_gon"] = jnp.tile(gdn_gon[l], HEADS)[None]
    lw["w_lru"] = wi[:, OFF_LRU:N_IN].astype(BF16)
    lw["lru_conv"] = lru_conv[l]
    lw["lru_conv_b"] = lru_conv_b[l][None]
    lw["lru_wa"] = jax.scipy.linalg.block_diag(*lru_wa[l]).astype(BF16)
    lw["lru_ba"] = lru_ba[l][None]
    lw["lru_wx"] = jax.scipy.linalg.block_diag(*lru_wx[l]).astype(BF16)
    lw["lru_bx"] = lru_bx[l][None]
    lw["lru_lam"] = lru_lam[l][None]
    lw["w_gate"] = w_gate[l].astype(BF16)
    lw["b_gate"] = b_gate[l][:, None, :]
    lw["w_branch"] = w_branch[l].astype(BF16)
    lw["w_out"] = w_out[l].astype(BF16)
    return lw


def _rope_tables(t):
    rel = (jnp.arange(t) - PAD_LEN).astype(F32)
    inv_freq = ROPE_BASE ** (-(jnp.arange(0, MLA_ROPE, 2, dtype=F32) / MLA_ROPE))
    ang = rel[:, None] * inv_freq[None, :]
    cos, sin = jnp.cos(ang), jnp.sin(ang)
    pad = jnp.zeros((t, 128 - MLA_NOPE - MLA_ROPE), F32)
    cos128 = jnp.concatenate([jnp.ones((t, MLA_NOPE), F32), cos, cos, pad], axis=-1)
    sin128 = jnp.concatenate([jnp.zeros((t, MLA_NOPE), F32), sin, sin, pad], axis=-1)
    return cos128, sin128


def _ffn_rows(n):
    for cand in (512, 256, 128, 64, 32, 16, 8):
        if n % cand == 0:
            return cand
    raise ValueError(f"row count {n} is not a multiple of 8")


def kernel(x, meta, ln_ffn1, ffn1_wi, ffn1_wo, ln_mix, w_in, fox_bf, mla_gq, mla_wq, mla_gkv, mla_wkv, gdn_conv, gdn_alog, gdn_dtb, gdn_gon, lru_conv, lru_conv_b, lru_wa, lru_ba, lru_wx, lru_bx, lru_lam, w_gate, b_gate, w_branch, w_out, ln_ffn2, ffn2_wi, ffn2_wo, ln_final):
    b, s, d = x.shape
    t = BLOCK + s
    depth = w_in.shape[0]
    h = jnp.concatenate([jnp.zeros((b, PAD_LEN, d), x.dtype),
                         jnp.broadcast_to(meta.astype(x.dtype)[None], (b, N_META, d)), x], axis=1)
    tt = _row_tile(t)
    tm = _ffn_rows(b * t)
    tq = next(c for c in (512, 256, 128) if s % c == 0)
    cos128, sin128 = _rope_tables(t)
    idx = jnp.arange(tt)
    ctri = ((idx[:, None] >= idx[None, :])
            & (idx[:, None] // GDN_CHUNK == idx[None, :] // GDN_CHUNK)).astype(BF16)
    lane_head = jnp.arange(HEADS * DH) // DH
    bd_ones = (lane_head[:, None] == lane_head[None, :]).astype(BF16)
    gf = ln_final[None]
    for l in range(depth):
        lw = _layer_weights(l, w_in, fox_bf, mla_gq, mla_wq, mla_gkv, mla_wkv, gdn_conv, gdn_alog,
                            gdn_dtb, gdn_gon, lru_conv, lru_conv_b, lru_wa, lru_ba, lru_wx, lru_bx,
                            lru_lam, w_gate, b_gate, w_branch, w_out, ln_mix)
        h = _ffn(h.reshape(b * t, d), ln_ffn1[l][None], ffn1_wi[l].astype(BF16),
                 ffn1_wo[l].astype(BF16), gf, tm=tm).reshape(b, t, d)
        (fq, fk, fv, fcum, mq, mk, mv, gqkv, gab, ggate, y_lru) = _inproj(
            h, lw, cos128, sin128, ctri, bd_ones, tt=tt)
        y_fox = _attention(fq, fk, fv, fcum, head_slices=False, tq=tq)
        y_mla = _attention(mq, mk, mv, None, head_slices=True, tq=tq)
        y_gdn = _gdn(gqkv, gab, ggate, lw, tg=tt)
        h = _merge(h, (y_fox, y_mla, y_gdn, y_lru), lw, tt=tt)
        wi2, wo2 = ffn2_wi[l].astype(BF16), ffn2_wo[l].astype(BF16)
        if l == depth - 1:
            return _ffn_out(h, ln_ffn2[l][None], wi2, wo2, gf, tf=tq)
        h = _ffn(h.reshape(b * t, d), ln_ffn2[l][None], wi2, wo2, gf, tm=tm).reshape(b, t, d)
```

```python
import functools
import math

import jax
import jax.numpy as jnp
from jax import lax
from jax.experimental import pallas as pl
from jax.experimental.pallas import tpu as pltpu

F32 = jnp.float32
BF16 = jnp.bfloat16

N_META = 16
BLOCK = 128
PAD_LEN = BLOCK - N_META
EPS = 1e-6
NEG_INF = -1e30
N_BRANCH = 4
D_BRANCH = 256
HEADS = 4
DH = 64
MLA_NOPE = 64
MLA_ROPE = 32
MLA_Q_RANK = 192
MLA_KV_RANK = 128
ROPE_BASE = 10000.0
GDN_CONV = 4
GDN_CHUNK = 64
LRU_CONV = 4
LRU_C = 8.0
LRU_SCAN = 32
HALO = 8
LOG2E = 1.4426950408889634

OFF_FOX_QKV = 0
OFF_FOX_F = OFF_FOX_QKV + 3 * HEADS * DH
OFF_MLA_CQ = OFF_FOX_F + HEADS
OFF_MLA_CKV = OFF_MLA_CQ + MLA_Q_RANK
OFF_MLA_KR = OFF_MLA_CKV + MLA_KV_RANK
OFF_GDN_QKV = OFF_MLA_KR + MLA_ROPE
OFF_GDN_A = OFF_GDN_QKV + HEADS * 3 * DH
OFF_GDN_B = OFF_GDN_A + HEADS
OFF_GDN_G = OFF_GDN_B + HEADS
OFF_LRU = OFF_GDN_G + HEADS * DH
N_IN = OFF_LRU + D_BRANCH

VMEM_LIMIT = 56 * 1024 * 1024


def _cparams(*sem):
    return pltpu.CompilerParams(dimension_semantics=sem, vmem_limit_bytes=VMEM_LIMIT)


def _resident(shape):
    nd = len(shape)
    return pl.BlockSpec(shape, lambda *_: (0,) * nd, pipeline_mode=pl.Buffered(1))


def _dot(a, b):
    return jnp.dot(a, b, preferred_element_type=F32)


def _dot_nt(a, b):
    return lax.dot_general(a, b, (((1,), (1,)), ((), ())), preferred_element_type=F32)


def _dot_tn(a, b):
    return lax.dot_general(a, b, (((0,), (0,)), ((), ())), preferred_element_type=F32)


def _split(x, pieces):
    out = []
    for _ in range(pieces - 1):
        hi = x.astype(BF16)
        out.append(hi)
        x = x - hi.astype(F32)
    out.append(x.astype(BF16))
    return out


def _dot_exact_rhs(x, m, pieces=3):
    return sum(_dot(p, m) for p in _split(x, pieces))


def _dot_exact_lhs(m, x, pieces=3):
    return sum(_dot(m, p) for p in _split(x, pieces))


def _rms(x, g):
    var = jnp.mean(x * x, axis=-1, keepdims=True)
    return x * lax.rsqrt(var + EPS) * g


def _sigmoid(x):
    return 1.0 / (1.0 + jnp.exp(-x))


def _softplus(x):
    return jnp.maximum(x, 0.0) + jnp.log1p(jnp.exp(-jnp.abs(x)))


def _iota(shape, dim):
    return lax.broadcasted_iota(jnp.int32, shape, dim)


def _row_tile(t):
    for cand in (384, 256, 512, 128):
        if t % cand == 0:
            return cand
    raise ValueError(f"T={t} has no supported row tile")


def _ffn_kernel(*refs, d_ff, final_norm):
    *x_refs, g_ref, wi_ref, wo_ref, gf_ref, o_ref = refs
    x = jnp.concatenate([r[...] for r in x_refs], axis=0) if len(x_refs) > 1 else x_refs[0][...]
    xn = _rms(x, g_ref[...]).astype(BF16)
    gu = _dot(xn, wi_ref[...])
    g = gu[:, :d_ff]
    u = gu[:, d_ff:]
    act = (g * _sigmoid(g) * u).astype(BF16)
    y = x + 0.5 * _dot(act, wo_ref[...])
    if final_norm:
        y = _rms(y, gf_ref[...])
    o_ref[...] = y


def _ffn(h2, g, wi, wo, gf, *, tm):
    n, d = h2.shape
    d_ff = wo.shape[0]
    return pl.pallas_call(
        functools.partial(_ffn_kernel, d_ff=d_ff, final_norm=False),
        grid=(n // tm,),
        in_specs=[pl.BlockSpec((tm, d), lambda i: (i, 0)),
                  _resident((1, d)), _resident(wi.shape), _resident(wo.shape), _resident((1, d))],
        out_specs=pl.BlockSpec((tm, d), lambda i: (i, 0)),
        out_shape=jax.ShapeDtypeStruct((n, d), F32),
        compiler_params=_cparams("parallel"),
        name="ffn",
    )(h2, g, wi, wo, gf)


def _ffn_out(h, g, wi, wo, gf, *, tf):
    b, t, d = h.shape
    d_ff = wo.shape[0]
    nblk = tf // BLOCK
    x_specs = [pl.BlockSpec((None, BLOCK, d), lambda i, j, k=k: (i, nblk * j + 1 + k, 0))
               for k in range(nblk)]
    return pl.pallas_call(
        functools.partial(_ffn_kernel, d_ff=d_ff, final_norm=True),
        grid=(b, (t - BLOCK) // tf),
        in_specs=x_specs + [_resident((1, d)), _resident(wi.shape), _resident(wo.shape),
                            _resident((1, d))],
        out_specs=pl.BlockSpec((None, tf, d), lambda i, j: (i, j, 0)),
        out_shape=jax.ShapeDtypeStruct((b, t - BLOCK, d), F32),
        compiler_params=_cparams("parallel", "parallel"),
        name="ffn_out",
    )(*([h] * nblk), g, wi, wo, gf)


def _inproj_kernel(h_ref, g_ref, wfox_ref, wft_ref, bf_ref, wmla_ref, gq_ref, wq_ref, wqs_ref,
                   gkv_ref, wkv_ref, cos_ref, sin_ref, wgdn_ref, wlru_ref,
                   cw_ref, alog_ref, dtb_ref, ctri_ref, bdones_ref,
                   lcw_ref, lcb_ref, lwa_ref, lba_ref, lwx_ref, lbx_ref, lam_ref,
                   fq_ref, fk_ref, fv_ref, fcum_ref, mq_ref, mk_ref, mv_ref,
                   gqkv_ref, gab_ref, ggate_ref, lru_ref, carry_ref, ext_ref, lext_ref, lh_ref,
                   *, tt):
    t = pl.program_id(1)
    w = HEADS * DH

    @pl.when(t == 0)
    def _():
        carry_ref[...] = jnp.zeros_like(carry_ref)
        ext_ref[0:HALO, :] = jnp.zeros((HALO, 3 * w), F32)
        lext_ref[0:HALO, :] = jnp.zeros((HALO, D_BRANCH), F32)
        lh_ref[...] = jnp.zeros_like(lh_ref)

    @pl.when(t > 0)
    def _():
        ext_ref[0:HALO, :] = ext_ref[tt:tt + HALO, :]
        lext_ref[0:HALO, :] = lext_ref[tt:tt + HALO, :]

    rows = t * tt + _iota((tt, 1), 0)
    valid = rows >= PAD_LEN
    u = jnp.where(valid, _rms(h_ref[...], g_ref[...]), 0.0).astype(BF16)

    pg = _dot(u, wgdn_ref[...])
    fox = _dot(u, wfox_ref[...])
    pm = _dot(u, wmla_ref[...])
    z = _dot_nt(wft_ref[...], u) + bf_ref[...]
    lext_ref[HALO:HALO + tt, :] = _dot(u, wlru_ref[...])

    ext_ref[HALO:HALO + tt, :] = pg[:, :3 * w]
    x_ext = ext_ref[...]
    conv = cw_ref[GDN_CONV - 1:GDN_CONV, :] * x_ext[HALO:, :]
    for d in range(1, GDN_CONV):
        kk = GDN_CONV - 1 - d
        conv = conv + cw_ref[kk:kk + 1, :] * pltpu.roll(x_ext, d, 0)[HALO:, :]
    qkv = conv * _sigmoid(conv)

    fq_ref[...] = (fox[:, :w] * (DH ** -0.5 * LOG2E)).astype(BF16)
    fk_ref[...] = fox[:, w:2 * w].astype(BF16)
    fv_ref[...] = fox[:, 2 * w:].astype(BF16)

    log_f = jnp.minimum(z, 0.0) - jnp.log1p(jnp.exp(-jnp.abs(z)))
    lane = _iota((8, 128), 1)
    running = carry_ref[...]
    for blk in range(tt // 128):
        x = log_f[:, blk * 128:(blk + 1) * 128]
        s = 1
        while s < 128:
            x = x + jnp.where(lane >= s, pltpu.roll(x, s, 1), 0.0)
            s *= 2
        x = x + running
        fcum_ref[:, blk * 128:(blk + 1) * 128] = x * LOG2E
        running = x[:, 127:128]
    carry_ref[...] = running

    cos = cos_ref[...]
    sin = sin_ref[...]
    cq = pm[:, :256]
    cqn = (cq * lax.rsqrt(jnp.sum(cq * cq, axis=-1, keepdims=True) / MLA_Q_RANK + EPS)
           * gq_ref[...]).astype(BF16)
    qa = _dot(cqn, wq_ref[...])
    qb = _dot(cqn, wqs_ref[...])
    ckv = pm[:, 256:384]
    ckvn = (_rms(ckv, gkv_ref[...])).astype(BF16)
    kv = _dot(ckvn, wkv_ref[...])
    kr = pm[:, 384:512] * cos + pm[:, 512:640] * sin
    scale = (MLA_NOPE + MLA_ROPE) ** -0.5 * LOG2E
    for hd in range(HEADS):
        sl = slice(hd * 128, (hd + 1) * 128)
        mq_ref[:, sl] = ((qa[:, sl] * cos + qb[:, sl] * sin) * scale).astype(BF16)
        mk_ref[:, sl] = (kv[:, sl] + kr).astype(BF16)
    mv_ref[...] = kv[:, HEADS * 128:].astype(BF16)

    bd_ones = bdones_ref[...]

    def l2n(x):
        return x * lax.rsqrt(_dot_exact_rhs(x * x, bd_ones, pieces=2) + EPS)

    gqkv_ref[:, 0:w] = l2n(qkv[:, :w]) * (DH ** -0.5)
    gqkv_ref[:, w:2 * w] = l2n(qkv[:, w:2 * w])
    gqkv_ref[:, 2 * w:] = qkv[:, 2 * w:]
    log_decay = -jnp.exp(alog_ref[...]) * _softplus(pg[:, 3 * w:4 * w] + dtb_ref[...])
    gab_ref[:, 0:w] = _dot_exact_lhs(ctri_ref[...], log_decay)
    gab_ref[:, w:] = _sigmoid(pg[:, 4 * w:5 * w])
    gate = pg[:, 5 * w:]
    ggate_ref[...] = gate * _sigmoid(gate)

    l_ext = lext_ref[...]
    xr = lcb_ref[...] + lcw_ref[LRU_CONV - 1:LRU_CONV, :] * l_ext[HALO:, :]
    for d in range(1, LRU_CONV):
        kk = LRU_CONV - 1 - d
        xr = xr + lcw_ref[kk:kk + 1, :] * pltpu.roll(l_ext, d, 0)[HALO:, :]
    xr = jnp.where(valid, xr, 0.0)
    xb = xr.astype(BF16)
    r = _sigmoid(_dot(xb, lwa_ref[...]) + lba_ref[...])
    ig = _sigmoid(_dot(xb, lwx_ref[...]) + lbx_ref[...])
    log_a = -LRU_C * r * _softplus(-lam_ref[...])
    a = jnp.exp(log_a)
    bv = jnp.sqrt(-jnp.tanh(log_a) * (a * a + 1.0)) * ig * xr
    local = _iota((tt, 1), 0) % LRU_SCAN
    s = 1
    while s < LRU_SCAN:
        keep = local >= s
        a_sh = jnp.where(keep, pltpu.roll(a, s, 0), 1.0)
        b_sh = jnp.where(keep, pltpu.roll(bv, s, 0), 0.0)
        bv = a * b_sh + bv
        a = a * a_sh
        s *= 2
    state = lh_ref[...]
    for blk in range(tt // LRU_SCAN):
        sl = slice(blk * LRU_SCAN, (blk + 1) * LRU_SCAN)
        hs = a[sl] * state + bv[sl]
        lru_ref[sl, :] = hs.astype(lru_ref.dtype)
        state = hs[LRU_SCAN - 1:LRU_SCAN, :]
    lh_ref[...] = state


def _inproj(h, lw, cos128, sin128, ctri, bd_ones, *, tt):
    b, t, d = h.shape
    w = HEADS * DH
    row = lambda c: pl.BlockSpec((None, tt, c), lambda i, j: (i, j, 0))
    ins = [h, lw["ln_mix"], lw["w_fox"], lw["w_ft"], lw["b_f"], lw["w_mla"], lw["g_q"], lw["w_q"],
           lw["w_qs"], lw["g_kv"], lw["w_kv"], cos128, sin128, lw["w_gdn"], lw["w_lru"],
           lw["gdn_conv"], lw["gdn_alog"], lw["gdn_dtb"], ctri, bd_ones,
           lw["lru_conv"], lw["lru_conv_b"], lw["lru_wa"], lw["lru_ba"], lw["lru_wx"], lw["lru_bx"],
           lw["lru_lam"]]
    in_specs = [row(d)] + [_resident(a.shape) for a in ins[1:11]]
    in_specs += [pl.BlockSpec((tt, 128), lambda i, j: (j, 0))] * 2
    in_specs += [_resident(a.shape) for a in ins[13:]]
    out_shape = [jax.ShapeDtypeStruct((b, t, w), BF16)] * 3
    out_shape += [jax.ShapeDtypeStruct((b, 8, t), F32)]
    out_shape += [jax.ShapeDtypeStruct((b, t, 512), BF16)] * 2 + [jax.ShapeDtypeStruct((b, t, w), BF16)]
    out_shape += [jax.ShapeDtypeStruct((b, t, 3 * w), F32), jax.ShapeDtypeStruct((b, t, 2 * w), F32),
                  jax.ShapeDtypeStruct((b, t, w), F32), jax.ShapeDtypeStruct((b, t, D_BRANCH), BF16)]
    out_specs = [row(w)] * 3 + [pl.BlockSpec((None, 8, tt), lambda i, j: (i, 0, j))]
    out_specs += [row(512)] * 2 + [row(w), row(3 * w), row(2 * w), row(w), row(w)]
    return pl.pallas_call(
        functools.partial(_inproj_kernel, tt=tt),
        grid=(b, t // tt),
        in_specs=in_specs, out_specs=out_specs, out_shape=out_shape,
        scratch_shapes=[pltpu.VMEM((8, 1), F32), pltpu.VMEM((tt + HALO, 3 * w), F32),
                        pltpu.VMEM((tt + HALO, D_BRANCH), F32), pltpu.VMEM((1, D_BRANCH), F32)],
        compiler_params=_cparams("parallel", "arbitrary"),
        name="inproj",
    )(*ins)


def _attn_kernel(*refs, tq, head_slices, has_bias):
    if has_bias:
        q_ref, k_ref, v_ref, bias_ref, o_ref, vp_ref, m_ref, acc_ref = refs
    else:
        q_ref, k_ref, v_ref, o_ref, vp_ref, m_ref, acc_ref = refs
        bias_ref = None
    t, wq = q_ref.shape
    wv = v_ref.shape[-1]
    half = wv // 2
    lane_head = _iota((1, wv), 1) // DH
    v = v_ref[...]
    for hd in range(HEADS):
        vp_ref[hd] = jnp.where(lane_head == hd, v, jnp.ones_like(v))

    def run_tile(q0, rows, steps):
        q = q_ref[pl.ds(q0, rows), :]
        if head_slices:
            qh = [q[:, hd * 128:(hd + 1) * 128] for hd in range(HEADS)]
        else:
            q_head = _iota((1, wq), 1) // DH
            qh = [jnp.where(q_head == hd, q, jnp.zeros_like(q)) for hd in range(HEADS)]
        def scores(k0, cols):
            out = []
            for hd in range(HEADS):
                if head_slices:
                    kblk = k_ref[pl.ds(k0, cols), hd * 128:(hd + 1) * 128]
                else:
                    kblk = k_ref[pl.ds(k0, cols), :]
                out.append(_dot_nt(qh[hd], kblk))
            return out

        def consume(sc, k0, cols, mask, first=False):
            probs = []
            for hd, s in enumerate(sc):
                if has_bias:
                    s = s - bias_ref[hd:hd + 1, pl.ds(k0, cols)]
                if mask is not None:
                    s = jnp.where(mask, s, NEG_INF)
                s_max = jnp.max(s, axis=-1, keepdims=True)
                if first:
                    m_new = jnp.broadcast_to(s_max, (rows, 128))
                    alpha = None
                else:
                    m_old = m_ref[hd, 0:rows, :]
                    m_new = jnp.maximum(m_old, s_max)
                    alpha = jnp.exp2(m_old - m_new)
                m_ref[hd, 0:rows, :] = m_new
                p = jnp.exp2(s - jnp.concatenate([m_new] * (cols // 128), axis=1))
                probs.append((p.astype(BF16), alpha))
            for hd, (p, alpha) in enumerate(probs):
                pv = _dot(p, vp_ref[hd, pl.ds(k0, cols), :])
                if first:
                    acc_ref[hd, 0:rows, :] = pv
                else:
                    acc_ref[hd, 0:rows, :] = (
                        acc_ref[hd, 0:rows, :] * jnp.concatenate([alpha] * (wv // 128), axis=1) + pv)

        steps(scores, consume)
        out = jnp.zeros((rows, wv), F32)
        for hd in range(HEADS):
            acc = acc_ref[hd, 0:rows, :]
            row_sum = jnp.concatenate([acc[:, half:], acc[:, :half]], axis=1)
            out = jnp.where(lane_head == hd, acc / row_sum, out)
        o_ref[pl.ds(q0, rows), :] = out.astype(o_ref.dtype)

    meta_keys = _iota((1, BLOCK), 1) >= PAD_LEN
    causal0 = (_iota((BLOCK, BLOCK), 1) <= _iota((BLOCK, BLOCK), 0)) & meta_keys
    run_tile(0, BLOCK,
             lambda scores, consume: consume(scores(0, BLOCK), 0, BLOCK, causal0, first=True))

    causal = _iota((tq, tq), 1) <= _iota((tq, tq), 0)

    def outer(i, carry):
        q0 = pl.multiple_of(BLOCK + i * tq, 128)
        block = lambda j: pl.multiple_of(BLOCK + j * tq, 128)

        def steps(scores, consume):
            consume(scores(0, BLOCK), 0, BLOCK, meta_keys, first=True)

            def pair(jj, c):
                k_a, k_b = block(2 * jj), block(2 * jj + 1)
                s_a, s_b = scores(k_a, tq), scores(k_b, tq)
                consume(s_a, k_a, tq, None)
                consume(s_b, k_b, tq, None)
                return c

            lax.fori_loop(0, i // 2, pair, 0)

            @pl.when(i % 2 == 0)
            def _():
                consume(scores(q0, tq), q0, tq, causal)

            @pl.when(i % 2 == 1)
            def _():
                k_a = block(i - 1)
                s_a, s_b = scores(k_a, tq), scores(q0, tq)
                consume(s_a, k_a, tq, None)
                consume(s_b, q0, tq, causal)

        run_tile(q0, tq, steps)
        return carry

    lax.fori_loop(0, (t - BLOCK) // tq, outer, 0)


def _attention(q, k, v, bias, *, head_slices, tq):
    b, t, wq = q.shape
    wk = k.shape[-1]
    wv = v.shape[-1]
    assert wv == 2 * 128 and (t - BLOCK) % tq == 0
    full = lambda c: pl.BlockSpec((None, t, c), lambda i: (i, 0, 0))
    ins = [q, k, v]
    in_specs = [full(wq), full(wk), full(wv)]
    if bias is not None:
        ins.append(bias)
        in_specs.append(pl.BlockSpec((None, 8, t), lambda i: (i, 0, 0)))
    return pl.pallas_call(
        functools.partial(_attn_kernel, tq=tq, head_slices=head_slices, has_bias=bias is not None),
        grid=(b,),
        in_specs=in_specs,
        out_specs=full(wv),
        out_shape=jax.ShapeDtypeStruct((b, t, wv), BF16),
        scratch_shapes=[pltpu.VMEM((HEADS, t, wv), BF16), pltpu.VMEM((HEADS, tq, 128), F32),
                        pltpu.VMEM((HEADS, tq, wv), F32)],
        compiler_params=_cparams("parallel"),
        name="attn_mla" if head_slices else "attn_fox",
    )(*ins)


def _gdn_kernel(qkv_ref, ab_ref, gate_ref, gon_ref, o_ref, o_s, state_ref, *, tg, nb,
                group_chunks):
    t = pl.program_id(1)
    c = GDN_CHUNK
    w = HEADS * DH

    @pl.when(t == 0)
    def _():
        state_ref[...] = jnp.zeros_like(state_ref)

    bd = (_iota((w, w), 0) // DH) == (_iota((w, w), 1) // DH)
    bd_ones = jnp.where(bd, 1.0, 0.0).astype(BF16)
    row = _iota((c, w), 0)
    col = _iota((c, w), 1) % DH
    eye_rep = row == col
    incl_rep = row >= col
    strict_rep = row > col
    tile4 = jnp.where(_iota((c, w), 0) == col, 1.0, 0.0).astype(BF16)

    def zmat(x):
        xb = x.astype(BF16)
        return jnp.where(bd, jnp.concatenate([xb] * HEADS, axis=0), jnp.zeros((w, w), BF16))

    def mm(x, z):
        return _dot(x.astype(BF16), z)

    def prep(where):
        each = lambda f, *xs: [f(*a) for a in zip(*xs)]
        qc = [qkv_ref[bi, pl.ds(r0, c), 0:w] for bi, r0 in where]
        kc = [qkv_ref[bi, pl.ds(r0, c), w:2 * w] for bi, r0 in where]
        bc = [ab_ref[bi, pl.ds(r0, c), w:2 * w] for bi, r0 in where]
        kb = each(lambda k, b: k * b, kc, bc)
        vb = each(lambda br, b: qkv_ref[br[0], pl.ds(br[1], c), 2 * w:3 * w] * b, where, bc)
        big_g = [ab_ref[bi, pl.ds(r0, c), 0:w] for bi, r0 in where]
        zk = [jnp.where(bd, _dot_tn(k.astype(BF16), tile4), 0.0).astype(BF16) for k in kc]
        stack = lambda x, y: jnp.concatenate([x, y], axis=0)
        kq = each(lambda k, q, z: mm(stack(k, q), z), kb, qc, zk)
        g_row = [jnp.sum(jnp.where(eye_rep, g, 0.0), axis=0, keepdims=True) for g in big_g]
        decay = each(lambda g, r: jnp.exp(jnp.where(incl_rep, g - r, NEG_INF)), big_g, g_row)
        qk = each(lambda x, d: jnp.where(incl_rep, x[c:] * d, 0.0), kq, decay)
        p = each(lambda x, d: jnp.where(strict_rep, -(x[:c] * d), 0.0), kq, decay)
        tinv = [jnp.where(eye_rep, 1.0, 0.0) for _ in p]
        n_sq = int(math.log2(c)) - 1
        for i in range(n_sq):
            both = each(lambda x, tv: mm(stack(x, tv), zmat(x)), p, tinv)
            tinv = each(lambda tv, r: tv + r[c:], tinv, both)
            p = [r[:c] for r in both]
        tinv = each(lambda tv, x: tv + mm(tv, zmat(x)), tinv, p)
        e_g = [jnp.exp(g) for g in big_g]
        wmat = each(lambda tv, k, e: mm(tv, zmat(k * e)), tinv, kb, e_g)
        umat = each(lambda tv, x: mm(tv, zmat(x)), tinv, vb)
        wq = each(lambda wm, q, e: jnp.concatenate([wm, q * e], axis=0).astype(BF16), wmat, qc, e_g)
        k_dec = each(lambda k, g: (k * jnp.exp(g[c - 1:c, :] - g)).astype(BF16), kc, big_g)
        g_last = [jnp.exp(g[c - 1:c, :]) for g in big_g]
        return list(zip(wq, umat, qk, k_dec, g_last))

    def group(gi, carry):
        base = gi * (group_chunks * c)
        r0s = [pl.multiple_of(base + j * c, c) for j in range(group_chunks)]
        pre = prep([(bi, r0) for r0 in r0s for bi in range(nb)])
        states = [state_ref[bi] for bi in range(nb)]
        for j, r0 in enumerate(r0s):
            for bi in range(nb):
                wq, umat, qk, k_dec, gl = pre[j * nb + bi]
                ws_qs = _dot(wq, states[bi].astype(BF16))
                v_new = umat - ws_qs[:c]
                o_s[bi, pl.ds(r0, c), :] = ws_qs[c:] + mm(qk, zmat(v_new))
                upd = _dot_tn(k_dec, v_new.astype(BF16))
                states[bi] = states[bi] * gl + jnp.where(bd, upd, 0.0)
        for bi in range(nb):
            state_ref[bi] = states[bi]
        return carry

    lax.fori_loop(0, tg // (group_chunks * c), group, 0)

    for bi in range(nb):
        o = o_s[bi]
        ms = _dot_exact_rhs(o * o, bd_ones, pieces=2) * (1.0 / DH)
        o_ref[bi] = (o * lax.rsqrt(ms + EPS) * gon_ref[...] * gate_ref[bi]).astype(o_ref.dtype)


def _gdn(gqkv, gab, ggate, lw, *, tg):
    b, t, _ = gqkv.shape
    w = HEADS * DH
    nb = 2 if b % 2 == 0 else 1
    row = lambda c: pl.BlockSpec((nb, tg, c), lambda i, j: (i, j, 0))
    return pl.pallas_call(
        functools.partial(_gdn_kernel, tg=tg, nb=nb, group_chunks=next(
            g for g in (6, 4, 3, 2, 1) if (tg // GDN_CHUNK) % g == 0)),
        grid=(b // nb, t // tg),
        in_specs=[row(3 * w), row(2 * w), row(w), _resident((1, w))],
        out_specs=row(w),
        out_shape=jax.ShapeDtypeStruct((b, t, w), BF16),
        scratch_shapes=[pltpu.VMEM((nb, tg, w), F32), pltpu.VMEM((nb, w, w), F32)],
        compiler_params=_cparams("parallel", "arbitrary"),
        name="gdn",
    )(gqkv, gab, ggate, lw["gdn_gon"])


def _merge_kernel(h_ref, y0_ref, y1_ref, y2_ref, y3_ref, g_ref, wg_ref, bg_ref, wb_ref, wo_ref,
                  o_ref, *, tt):
    t = pl.program_id(1)
    h = h_ref[...]
    rows = t * tt + _iota((tt, 1), 0)
    u = jnp.where(rows >= PAD_LEN, _rms(h, g_ref[...]), 0.0).astype(BF16)
    merged = None
    for n, y_ref in enumerate((y0_ref, y1_ref, y2_ref, y3_ref)):
        gate = _sigmoid(_dot(u, wg_ref[n]) + bg_ref[n])
        term = gate * _dot(y_ref[...], wb_ref[n])
        merged = term if merged is None else merged + term
    o_ref[...] = h + _dot(merged.astype(BF16), wo_ref[...])


def _merge(h, ys, lw, *, tt):
    b, t, d = h.shape
    row = lambda c: pl.BlockSpec((None, tt, c), lambda i, j: (i, j, 0))
    return pl.pallas_call(
        functools.partial(_merge_kernel, tt=tt),
        grid=(b, t // tt),
        in_specs=[row(d)] + [row(D_BRANCH)] * N_BRANCH + [
            _resident((1, d)), _resident(lw["w_gate"].shape), _resident(lw["b_gate"].shape),
            _resident(lw["w_branch"].shape), _resident((d, d))],
        out_specs=row(d),
        out_shape=jax.ShapeDtypeStruct((b, t, d), F32),
        compiler_params=_cparams("parallel", "parallel"),
        name="merge",
    )(h, *ys, lw["ln_mix"], lw["w_gate"], lw["b_gate"], lw["w_branch"], lw["w_out"])


def _rope_swap(wr):
    half = wr.shape[-1] // 2
    return jnp.concatenate([-wr[..., half:], wr[..., :half]], axis=-1)


def _head_blocks(parts, width=128):
    rows = parts[0].shape[0]
    x = jnp.concatenate(parts, axis=-1)
    x = jnp.pad(x, ((0, 0), (0, 0), (0, width - x.shape[-1])))
    return x.reshape(rows, HEADS * width)


def _layer_weights(l, w_in, fox_bf, mla_gq, mla_wq, mla_gkv, mla_wkv, gdn_conv, gdn_alog, gdn_dtb,
                   gdn_gon, lru_conv, lru_conv_b, lru_wa, lru_ba, lru_wx, lru_bx, lru_lam,
                   w_gate, b_gate, w_branch, w_out, ln_mix):
    d = w_in.shape[1]
    wi = w_in[l]
    w = HEADS * DH
    lw = {"ln_mix": ln_mix[l][None]}
    lw["w_fox"] = wi[:, OFF_FOX_QKV:OFF_FOX_F].astype(BF16)
    lw["w_ft"] = jnp.pad(wi[:, OFF_FOX_F:OFF_MLA_CQ].T, ((0, 8 - HEADS), (0, 0))).astype(BF16)
    lw["b_f"] = jnp.pad(fox_bf[l], (0, 8 - HEADS))[:, None]
    z64 = jnp.zeros((d, 64), F32)
    z32 = jnp.zeros((d, 32), F32)
    kr = wi[:, OFF_MLA_KR:OFF_GDN_QKV]
    lw["w_mla"] = jnp.concatenate(
        [wi[:, OFF_MLA_CQ:OFF_MLA_CKV], z64, wi[:, OFF_MLA_CKV:OFF_MLA_KR],
         z64, kr, z32, z64, _rope_swap(kr), z32], axis=-1).astype(BF16)
    lw["g_q"] = jnp.pad(mla_gq[l], (0, 256 - MLA_Q_RANK))[None]
    wq = mla_wq[l].reshape(MLA_Q_RANK, HEADS, MLA_NOPE + MLA_ROPE)
    wq_nope, wq_rope = wq[..., :MLA_NOPE], wq[..., MLA_NOPE:]
    padq = lambda x: jnp.pad(x, ((0, 256 - MLA_Q_RANK), (0, 0))).astype(BF16)
    lw["w_q"] = padq(_head_blocks([wq_nope, wq_rope]))
    lw["w_qs"] = padq(_head_blocks([jnp.zeros_like(wq_nope), _rope_swap(wq_rope)]))
    lw["g_kv"] = mla_gkv[l][None]
    wkv = mla_wkv[l].reshape(MLA_KV_RANK, HEADS, MLA_NOPE + DH)
    lw["w_kv"] = jnp.concatenate(
        [_head_blocks([wkv[..., :MLA_NOPE]]), wkv[..., MLA_NOPE:].reshape(MLA_KV_RANK, w)],
        axis=-1).astype(BF16)
    rep = lambda x: jnp.repeat(x, DH, axis=-1)
    lw["w_gdn"] = jnp.concatenate(
        [wi[:, OFF_GDN_QKV:OFF_GDN_A], rep(wi[:, OFF_GDN_A:OFF_GDN_B]),
         rep(wi[:, OFF_GDN_B:OFF_GDN_G]), wi[:, OFF_GDN_G:OFF_LRU]], axis=-1).astype(BF16)
    lw["gdn_conv"] = gdn_conv[l]
    lw["gdn_alog"] = rep(gdn_alog[l])[None]
    lw["gdn_dtb"] = rep(gdn_dtb[l])[None]
    lw["gdn_gon"] = jnp.tile(gdn_gon[l], HEADS)[None]
    lw["w_lru"] = wi[:, OFF_LRU:N_IN].astype(BF16)
    lw["lru_conv"] = lru_conv[l]
    lw["lru_conv_b"] = lru_conv_b[l][None]
    lw["lru_wa"] = jax.scipy.linalg.block_diag(*lru_wa[l]).astype(BF16)
    lw["lru_ba"] = lru_ba[l][None]
    lw["lru_wx"] = jax.scipy.linalg.block_diag(*lru_wx[l]).astype(BF16)
    lw["lru_bx"] = lru_bx[l][None]
    lw["lru_lam"] = lru_lam[l][None]
    lw["w_gate"] = w_gate[l].astype(BF16)
    lw["b_gate"] = b_gate[l][:, None, :]
    lw["w_branch"] = w_branch[l].astype(BF16)
    lw["w_out"] = w_out[l].astype(BF16)
    return lw


def _rope_tables(t):
    rel = (jnp.arange(t) - PAD_LEN).astype(F32)
    inv_freq = ROPE_BASE ** (-(jnp.arange(0, MLA_ROPE, 2, dtype=F32) / MLA_ROPE))
    ang = rel[:, None] * inv_freq[None, :]
    cos, sin = jnp.cos(ang), jnp.sin(ang)
    pad = jnp.zeros((t, 128 - MLA_NOPE - MLA_ROPE), F32)
    cos128 = jnp.concatenate([jnp.ones((t, MLA_NOPE), F32), cos, cos, pad], axis=-1)
    sin128 = jnp.concatenate([jnp.zeros((t, MLA_NOPE), F32), sin, sin, pad], axis=-1)
    return cos128, sin128


def _ffn_rows(n):
    for cand in (512, 256, 128, 64, 32, 16, 8):
        if n % cand == 0:
            return cand
    raise ValueError(f"row count {n} is not a multiple of 8")


def kernel(x, meta, ln_ffn1, ffn1_wi, ffn1_wo, ln_mix, w_in, fox_bf, mla_gq, mla_wq, mla_gkv, mla_wkv, gdn_conv, gdn_alog, gdn_dtb, gdn_gon, lru_conv, lru_conv_b, lru_wa, lru_ba, lru_wx, lru_bx, lru_lam, w_gate, b_gate, w_branch, w_out, ln_ffn2, ffn2_wi, ffn2_wo, ln_final):
    b, s, d = x.shape
    t = BLOCK + s
    depth = w_in.shape[0]
    h = jnp.concatenate([jnp.zeros((b, PAD_LEN, d), x.dtype),
                         jnp.broadcast_to(meta.astype(x.dtype)[None], (b, N_META, d)), x], axis=1)
    tt = _row_tile(t)
    tm = _ffn_rows(b * t)
    tq = next(c for c in (512, 256, 128) if s % c == 0)
    cos128, sin128 = _rope_tables(t)
    idx = jnp.arange(tt)
    ctri = ((idx[:, None] >= idx[None, :])
            & (idx[:, None] // GDN_CHUNK == idx[None, :] // GDN_CHUNK)).astype(BF16)
    lane_head = jnp.arange(HEADS * DH) // DH
    bd_ones = (lane_head[:, None] == lane_head[None, :]).astype(BF16)
    gf = ln_final[None]
    for l in range(depth):
        lw = _layer_weights(l, w_in, fox_bf, mla_gq, mla_wq, mla_gkv, mla_wkv, gdn_conv, gdn_alog,
                            gdn_dtb, gdn_gon, lru_conv, lru_conv_b, lru_wa, lru_ba, lru_wx, lru_bx,
                            lru_lam, w_gate, b_gate, w_branch, w_out, ln_mix)
        h = _ffn(h.reshape(b * t, d), ln_ffn1[l][None], ffn1_wi[l].astype(BF16),
                 ffn1_wo[l].astype(BF16), gf, tm=tm).reshape(b, t, d)
        (fq, fk, fv, fcum, mq, mk, mv, gqkv, gab, ggate, y_lru) = _inproj(
            h, lw, cos128, sin128, ctri, bd_ones, tt=tt)
        y_fox = _attention(fq, fk, fv, fcum, head_slices=False, tq=tq)
        y_mla = _attention(mq, mk, mv, None, head_slices=True, tq=tq)
        y_gdn = _gdn(gqkv, gab, ggate, lw, tg=tt)
        h = _merge(h, (y_fox, y_mla, y_gdn, y_lru), lw, tt=tt)
        wi2, wo2 = ffn2_wi[l].astype(BF16), ffn2_wo[l].astype(BF16)
        if l == depth - 1:
            return _ffn_out(h, ln_ffn2[l][None], wi2, wo2, gf, tf=tq)
        h = _ffn(h.reshape(b * t, d), ln_ffn2[l][None], wi2, wo2, gf, tm=tm).reshape(b, t, d)
```

```python
import functools
import math

import jax
import jax.numpy as jnp
from jax import lax
from jax.experimental import pallas as pl
from jax.experimental.pallas import tpu as pltpu

F32 = jnp.float32
BF16 = jnp.bfloat16

N_META = 16
BLOCK = 128
PAD_LEN = BLOCK - N_META
EPS = 1e-6
NEG_INF = -1e30
N_BRANCH = 4
D_BRANCH = 256
HEADS = 4
DH = 64
MLA_NOPE = 64
MLA_ROPE = 32
MLA_Q_RANK = 192
MLA_KV_RANK = 128
ROPE_BASE = 10000.0
GDN_CONV = 4
GDN_CHUNK = 64
LRU_CONV = 4
LRU_C = 8.0
LRU_SCAN = 32
HALO = 8
LOG2E = 1.4426950408889634

OFF_FOX_QKV = 0
OFF_FOX_F = OFF_FOX_QKV + 3 * HEADS * DH
OFF_MLA_CQ = OFF_FOX_F + HEADS
OFF_MLA_CKV = OFF_MLA_CQ + MLA_Q_RANK
OFF_MLA_KR = OFF_MLA_CKV + MLA_KV_RANK
OFF_GDN_QKV = OFF_MLA_KR + MLA_ROPE
OFF_GDN_A = OFF_GDN_QKV + HEADS * 3 * DH
OFF_GDN_B = OFF_GDN_A + HEADS
OFF_GDN_G = OFF_GDN_B + HEADS
OFF_LRU = OFF_GDN_G + HEADS * DH
N_IN = OFF_LRU + D_BRANCH

VMEM_LIMIT = 56 * 1024 * 1024


def _cparams(*sem):
    return pltpu.CompilerParams(dimension_semantics=sem, vmem_limit_bytes=VMEM_LIMIT)


def _resident(shape):
    nd = len(shape)
    return pl.BlockSpec(shape, lambda *_: (0,) * nd, pipeline_mode=pl.Buffered(1))


def _dot(a, b):
    return jnp.dot(a, b, preferred_element_type=F32)


def _dot_nt(a, b):
    return lax.dot_general(a, b, (((1,), (1,)), ((), ())), preferred_element_type=F32)


def _dot_tn(a, b):
    return lax.dot_general(a, b, (((0,), (0,)), ((), ())), preferred_element_type=F32)


def _split(x, pieces):
    out = []
    for _ in range(pieces - 1):
        hi = x.astype(BF16)
        out.append(hi)
        x = x - hi.astype(F32)
    out.append(x.astype(BF16))
    return out


def _dot_exact_rhs(x, m, pieces=3):
    return sum(_dot(p, m) for p in _split(x, pieces))


def _dot_exact_lhs(m, x, pieces=3):
    return sum(_dot(m, p) for p in _split(x, pieces))


def _rms(x, g):
    var = jnp.mean(x * x, axis=-1, keepdims=True)
    return x * lax.rsqrt(var + EPS) * g


def _sigmoid(x):
    return 1.0 / (1.0 + jnp.exp(-x))


def _softplus(x):
    return jnp.maximum(x, 0.0) + jnp.log1p(jnp.exp(-jnp.abs(x)))


def _iota(shape, dim):
    return lax.broadcasted_iota(jnp.int32, shape, dim)


def _row_tile(t):
    for cand in (384, 256, 512, 128):
        if t % cand == 0:
            return cand
    raise ValueError(f"T={t} has no supported row tile")


def _merge_tile(t, tt):
    return max([c for c in range(16, 769, 16) if t % c == 0] + [tt])


def _ffn_kernel(*refs, d_ff, final_norm):
    *x_refs, g_ref, wi_ref, wo_ref, gf_ref, o_ref = refs
    x = jnp.concatenate([r[...] for r in x_refs], axis=0) if len(x_refs) > 1 else x_refs[0][...]
    rows = x.shape[0] // 2
    xs = [x[:rows], x[rows:]]
    xn = [_rms(v, g_ref[...]).astype(BF16) for v in xs]
    gu = [_dot(v, wi_ref[...]) for v in xn]
    act = [(v[:, :d_ff] * _sigmoid(v[:, :d_ff]) * v[:, d_ff:]).astype(BF16) for v in gu]
    ys = [v + 0.5 * _dot(a, wo_ref[...]) for v, a in zip(xs, act)]
    for k, y in enumerate(ys):
        if final_norm:
            y = _rms(y, gf_ref[...])
        o_ref[k * rows:(k + 1) * rows, :] = y


def _ffn(h2, g, wi, wo, gf, *, tm):
    n, d = h2.shape
    d_ff = wo.shape[0]
    return pl.pallas_call(
        functools.partial(_ffn_kernel, d_ff=d_ff, final_norm=False),
        grid=(n // tm,),
        in_specs=[pl.BlockSpec((tm, d), lambda i: (i, 0)),
                  _resident((1, d)), _resident(wi.shape), _resident(wo.shape), _resident((1, d))],
        out_specs=pl.BlockSpec((tm, d), lambda i: (i, 0)),
        out_shape=jax.ShapeDtypeStruct((n, d), F32),
        compiler_params=_cparams("parallel"),
        name="ffn",
    )(h2, g, wi, wo, gf)


def _ffn_out(h, g, wi, wo, gf, *, tf):
    b, t, d = h.shape
    d_ff = wo.shape[0]
    nblk = tf // BLOCK
    x_specs = [pl.BlockSpec((None, BLOCK, d), lambda i, j, k=k: (i, nblk * j + 1 + k, 0))
               for k in range(nblk)]
    return pl.pallas_call(
        functools.partial(_ffn_kernel, d_ff=d_ff, final_norm=True),
        grid=(b, (t - BLOCK) // tf),
        in_specs=x_specs + [_resident((1, d)), _resident(wi.shape), _resident(wo.shape),
                            _resident((1, d))],
        out_specs=pl.BlockSpec((None, tf, d), lambda i, j: (i, j, 0)),
        out_shape=jax.ShapeDtypeStruct((b, t - BLOCK, d), F32),
        compiler_params=_cparams("parallel", "parallel"),
        name="ffn_out",
    )(*([h] * nblk), g, wi, wo, gf)


def _inproj_kernel(h_ref, g_ref, wfox_ref, wft_ref, bf_ref, wmla_ref, gq_ref, wq_ref, wqs_ref,
                   gkv_ref, wkv_ref, cos_ref, sin_ref, wgdn_ref, wlru_ref,
                   cw_ref, alog_ref, dtb_ref, ctri_ref, bdones_ref,
                   lcw_ref, lcb_ref, lwa_ref, lba_ref, lwx_ref, lbx_ref, lam_ref,
                   fq_ref, fk_ref, fv_ref, fcum_ref, mq_ref, mk_ref, mv_ref,
                   gqkv_ref, gab_ref, ggate_ref, lru_ref, carry_ref, ext_ref, lext_ref, lh_ref,
                   *, tt):
    t = pl.program_id(1)
    w = HEADS * DH

    @pl.when(t == 0)
    def _():
        carry_ref[...] = jnp.zeros_like(carry_ref)
        ext_ref[0:HALO, :] = jnp.zeros((HALO, 3 * w), F32)
        lext_ref[0:HALO, :] = jnp.zeros((HALO, D_BRANCH), F32)
        lh_ref[...] = jnp.zeros_like(lh_ref)

    @pl.when(t > 0)
    def _():
        ext_ref[0:HALO, :] = ext_ref[tt:tt + HALO, :]
        lext_ref[0:HALO, :] = lext_ref[tt:tt + HALO, :]

    rows = t * tt + _iota((tt, 1), 0)
    valid = rows >= PAD_LEN
    u = jnp.where(valid, _rms(h_ref[...], g_ref[...]), 0.0).astype(BF16)

    pg = _dot(u, wgdn_ref[...])
    fox = _dot(u, wfox_ref[...])
    pm = _dot(u, wmla_ref[...])
    z = _dot_nt(wft_ref[...], u) + bf_ref[...]
    lext_ref[HALO:HALO + tt, :] = _dot(u, wlru_ref[...])

    ext_ref[HALO:HALO + tt, :] = pg[:, :3 * w]
    x_ext = ext_ref[...]
    conv = cw_ref[GDN_CONV - 1:GDN_CONV, :] * x_ext[HALO:, :]
    for d in range(1, GDN_CONV):
        kk = GDN_CONV - 1 - d
        conv = conv + cw_ref[kk:kk + 1, :] * pltpu.roll(x_ext, d, 0)[HALO:, :]
    qkv = conv * _sigmoid(conv)

    fq_ref[...] = (fox[:, :w] * (DH ** -0.5 * LOG2E)).astype(BF16)
    fk_ref[...] = fox[:, w:2 * w].astype(BF16)
    fv_ref[...] = fox[:, 2 * w:].astype(BF16)

    log_f = jnp.minimum(z, 0.0) - jnp.log1p(jnp.exp(-jnp.abs(z)))
    lane = _iota((8, 128), 1)
    running = carry_ref[...]
    for blk in range(tt // 128):
        x = log_f[:, blk * 128:(blk + 1) * 128]
        s = 1
        while s < 128:
            x = x + jnp.where(lane >= s, pltpu.roll(x, s, 1), 0.0)
            s *= 2
        x = x + running
        fcum_ref[:, blk * 128:(blk + 1) * 128] = x * LOG2E
        running = x[:, 127:128]
    carry_ref[...] = running

    cos = cos_ref[...]
    sin = sin_ref[...]
    cq = pm[:, :256]
    cqn = (cq * lax.rsqrt(jnp.sum(cq * cq, axis=-1, keepdims=True) / MLA_Q_RANK + EPS)
           * gq_ref[...]).astype(BF16)
    qa = _dot(cqn, wq_ref[...])
    qb = _dot(cqn, wqs_ref[...])
    ckv = pm[:, 256:384]
    ckvn = (_rms(ckv, gkv_ref[...])).astype(BF16)
    kv = _dot(ckvn, wkv_ref[...])
    kr = pm[:, 384:512] * cos + pm[:, 512:640] * sin
    scale = (MLA_NOPE + MLA_ROPE) ** -0.5 * LOG2E
    for hd in range(HEADS):
        sl = slice(hd * 128, (hd + 1) * 128)
        mq_ref[:, sl] = ((qa[:, sl] * cos + qb[:, sl] * sin) * scale).astype(BF16)
        mk_ref[:, sl] = (kv[:, sl] + kr).astype(BF16)
    mv_ref[...] = kv[:, HEADS * 128:].astype(BF16)

    bd_ones = bdones_ref[...]

    def l2n(x):
        return x * lax.rsqrt(_dot_exact_rhs(x * x, bd_ones, pieces=2) + EPS)

    gqkv_ref[:, 0:w] = l2n(qkv[:, :w]) * (DH ** -0.5)
    gqkv_ref[:, w:2 * w] = l2n(qkv[:, w:2 * w])
    gqkv_ref[:, 2 * w:] = qkv[:, 2 * w:]
    log_decay = -jnp.exp(alog_ref[...]) * _softplus(pg[:, 3 * w:4 * w] + dtb_ref[...])
    gab_ref[:, 0:w] = _dot_exact_lhs(ctri_ref[...], log_decay)
    gab_ref[:, w:] = _sigmoid(pg[:, 4 * w:5 * w])
    gate = pg[:, 5 * w:]
    ggate_ref[...] = gate * _sigmoid(gate)

    l_ext = lext_ref[...]
    xr = lcb_ref[...] + lcw_ref[LRU_CONV - 1:LRU_CONV, :] * l_ext[HALO:, :]
    for d in range(1, LRU_CONV):
        kk = LRU_CONV - 1 - d
        xr = xr + lcw_ref[kk:kk + 1, :] * pltpu.roll(l_ext, d, 0)[HALO:, :]
    xr = jnp.where(valid, xr, 0.0)
    xb = xr.astype(BF16)
    r = _sigmoid(_dot(xb, lwa_ref[...]) + lba_ref[...])
    ig = _sigmoid(_dot(xb, lwx_ref[...]) + lbx_ref[...])
    log_a = -LRU_C * r * _softplus(-lam_ref[...])
    a = jnp.exp(log_a)
    bv = jnp.sqrt(-jnp.tanh(log_a) * (a * a + 1.0)) * ig * xr
    local = _iota((tt, 1), 0) % LRU_SCAN
    s = 1
    while s < LRU_SCAN:
        keep = local >= s
        a_sh = jnp.where(keep, pltpu.roll(a, s, 0), 1.0)
        b_sh = jnp.where(keep, pltpu.roll(bv, s, 0), 0.0)
        bv = a * b_sh + bv
        a = a * a_sh
        s *= 2
    state = lh_ref[...]
    for blk in range(tt // LRU_SCAN):
        sl = slice(blk * LRU_SCAN, (blk + 1) * LRU_SCAN)
        hs = a[sl] * state + bv[sl]
        lru_ref[sl, :] = hs.astype(lru_ref.dtype)
        state = hs[LRU_SCAN - 1:LRU_SCAN, :]
    lh_ref[...] = state


def _inproj(h, lw, cos128, sin128, ctri, bd_ones, *, tt):
    b, t, d = h.shape
    w = HEADS * DH
    row = lambda c: pl.BlockSpec((None, tt, c), lambda i, j: (i, j, 0))
    ins = [h, lw["ln_mix"], lw["w_fox"], lw["w_ft"], lw["b_f"], lw["w_mla"], lw["g_q"], lw["w_q"],
           lw["w_qs"], lw["g_kv"], lw["w_kv"], cos128, sin128, lw["w_gdn"], lw["w_lru"],
           lw["gdn_conv"], lw["gdn_alog"], lw["gdn_dtb"], ctri, bd_ones,
           lw["lru_conv"], lw["lru_conv_b"], lw["lru_wa"], lw["lru_ba"], lw["lru_wx"], lw["lru_bx"],
           lw["lru_lam"]]
    in_specs = [row(d)] + [_resident(a.shape) for a in ins[1:11]]
    in_specs += [pl.BlockSpec((tt, 128), lambda i, j: (j, 0))] * 2
    in_specs += [_resident(a.shape) for a in ins[13:]]
    out_shape = [jax.ShapeDtypeStruct((b, t, w), BF16)] * 3
    out_shape += [jax.ShapeDtypeStruct((b, 8, t), F32)]
    out_shape += [jax.ShapeDtypeStruct((b, t, 512), BF16)] * 2 + [jax.ShapeDtypeStruct((b, t, w), BF16)]
    out_shape += [jax.ShapeDtypeStruct((b, t, 3 * w), F32), jax.ShapeDtypeStruct((b, t, 2 * w), F32),
                  jax.ShapeDtypeStruct((b, t, w), F32), jax.ShapeDtypeStruct((b, t, D_BRANCH), BF16)]
    out_specs = [row(w)] * 3 + [pl.BlockSpec((None, 8, tt), lambda i, j: (i, 0, j))]
    out_specs += [row(512)] * 2 + [row(w), row(3 * w), row(2 * w), row(w), row(w)]
    return pl.pallas_call(
        functools.partial(_inproj_kernel, tt=tt),
        grid=(b, t // tt),
        in_specs=in_specs, out_specs=out_specs, out_shape=out_shape,
        scratch_shapes=[pltpu.VMEM((8, 1), F32), pltpu.VMEM((tt + HALO, 3 * w), F32),
                        pltpu.VMEM((tt + HALO, D_BRANCH), F32), pltpu.VMEM((1, D_BRANCH), F32)],
        compiler_params=_cparams("parallel", "arbitrary"),
        name="inproj",
    )(*ins)


def _attn_kernel(*refs, tq, head_slices, has_bias):
    if has_bias:
        q_ref, k_ref, v_ref, bias_ref, o_ref, vp_ref, m_ref, acc_ref = refs
    else:
        q_ref, k_ref, v_ref, o_ref, vp_ref, m_ref, acc_ref = refs
        bias_ref = None
    t, wq = q_ref.shape
    wv = v_ref.shape[-1]
    half = wv // 2
    lane_head = _iota((1, wv), 1) // DH
    v = v_ref[...]
    for hd in range(HEADS):
        vp_ref[hd] = jnp.where(lane_head == hd, v, jnp.ones_like(v))

    def run_tile(q0, rows, steps):
        q = q_ref[pl.ds(q0, rows), :]
        if head_slices:
            qh = [q[:, hd * 128:(hd + 1) * 128] for hd in range(HEADS)]
        else:
            q_head = _iota((1, wq), 1) // DH
            qh = [jnp.where(q_head == hd, q, jnp.zeros_like(q)) for hd in range(HEADS)]
        def scores(k0, cols):
            out = []
            for hd in range(HEADS):
                if head_slices:
                    kblk = k_ref[pl.ds(k0, cols), hd * 128:(hd + 1) * 128]
                else:
                    kblk = k_ref[pl.ds(k0, cols), :]
                out.append(_dot_nt(qh[hd], kblk))
            return out

        def consume(sc, k0, cols, mask, first=False):
            probs = []
            for hd, s in enumerate(sc):
                if has_bias:
                    s = s - bias_ref[hd:hd + 1, pl.ds(k0, cols)]
                if mask is not None:
                    s = jnp.where(mask, s, NEG_INF)
                s_max = jnp.max(s, axis=-1, keepdims=True)
                if first:
                    m_new = jnp.broadcast_to(s_max, (rows, 128))
                    alpha = None
                else:
                    m_old = m_ref[hd, 0:rows, :]
                    m_new = jnp.maximum(m_old, s_max)
                    alpha = jnp.exp2(m_old - m_new)
                m_ref[hd, 0:rows, :] = m_new
                p = jnp.exp2(s - jnp.concatenate([m_new] * (cols // 128), axis=1))
                probs.append((p.astype(BF16), alpha))
            for hd, (p, alpha) in enumerate(probs):
                pv = _dot(p, vp_ref[hd, pl.ds(k0, cols), :])
                if first:
                    acc_ref[hd, 0:rows, :] = pv
                else:
                    acc_ref[hd, 0:rows, :] = (
                        acc_ref[hd, 0:rows, :] * jnp.concatenate([alpha] * (wv // 128), axis=1) + pv)

        steps(scores, consume)
        out = jnp.zeros((rows, wv), F32)
        for hd in range(HEADS):
            acc = acc_ref[hd, 0:rows, :]
            row_sum = jnp.concatenate([acc[:, half:], acc[:, :half]], axis=1)
            out = jnp.where(lane_head == hd, acc / row_sum, out)
        o_ref[pl.ds(q0, rows), :] = out.astype(o_ref.dtype)

    meta_keys = _iota((1, BLOCK), 1) >= PAD_LEN
    causal0 = (_iota((BLOCK, BLOCK), 1) <= _iota((BLOCK, BLOCK), 0)) & meta_keys
    run_tile(0, BLOCK,
             lambda scores, consume: consume(scores(0, BLOCK), 0, BLOCK, causal0, first=True))

    causal = _iota((tq, tq), 1) <= _iota((tq, tq), 0)

    def outer(i, carry):
        q0 = pl.multiple_of(BLOCK + i * tq, 128)
        block = lambda j: pl.multiple_of(BLOCK + j * tq, 128)

        def steps(scores, consume):
            consume(scores(0, BLOCK), 0, BLOCK, meta_keys, first=True)

            def pair(jj, c):
                k_a, k_b = block(2 * jj), block(2 * jj + 1)
                s_a, s_b = scores(k_a, tq), scores(k_b, tq)
                consume(s_a, k_a, tq, None)
                consume(s_b, k_b, tq, None)
                return c

            lax.fori_loop(0, i // 2, pair, 0)

            @pl.when(i % 2 == 0)
            def _():
                consume(scores(q0, tq), q0, tq, causal)

            @pl.when(i % 2 == 1)
            def _():
                k_a = block(i - 1)
                s_a, s_b = scores(k_a, tq), scores(q0, tq)
                consume(s_a, k_a, tq, None)
                consume(s_b, q0, tq, causal)

        run_tile(q0, tq, steps)
        return carry

    lax.fori_loop(0, (t - BLOCK) // tq, outer, 0)


def _attention(q, k, v, bias, *, head_slices, tq):
    b, t, wq = q.shape
    wk = k.shape[-1]
    wv = v.shape[-1]
    assert wv == 2 * 128 and (t - BLOCK) % tq == 0
    full = lambda c: pl.BlockSpec((None, t, c), lambda i: (i, 0, 0))
    ins = [q, k, v]
    in_specs = [full(wq), full(wk), full(wv)]
    if bias is not None:
        ins.append(bias)
        in_specs.append(pl.BlockSpec((None, 8, t), lambda i: (i, 0, 0)))
    return pl.pallas_call(
        functools.partial(_attn_kernel, tq=tq, head_slices=head_slices, has_bias=bias is not None),
        grid=(b,),
        in_specs=in_specs,
        out_specs=full(wv),
        out_shape=jax.ShapeDtypeStruct((b, t, wv), BF16),
        scratch_shapes=[pltpu.VMEM((HEADS, t, wv), BF16), pltpu.VMEM((HEADS, tq, 128), F32),
                        pltpu.VMEM((HEADS, tq, wv), F32)],
        compiler_params=_cparams("parallel"),
        name="attn_mla" if head_slices else "attn_fox",
    )(*ins)


def _gdn_kernel(qkv_ref, ab_ref, gate_ref, gon_ref, o_ref, o_s, state_ref, *, tg, nb,
                group_chunks):
    t = pl.program_id(1)
    c = GDN_CHUNK
    w = HEADS * DH

    @pl.when(t == 0)
    def _():
        state_ref[...] = jnp.zeros_like(state_ref)

    bd = (_iota((w, w), 0) // DH) == (_iota((w, w), 1) // DH)
    bd_ones = jnp.where(bd, 1.0, 0.0).astype(BF16)
    row = _iota((c, w), 0)
    col = _iota((c, w), 1) % DH
    eye_rep = row == col
    incl_rep = row >= col
    strict_rep = row > col
    tile4 = jnp.where(_iota((c, w), 0) == col, 1.0, 0.0).astype(BF16)

    def zmat(x):
        xb = x.astype(BF16)
        return jnp.where(bd, jnp.concatenate([xb] * HEADS, axis=0), jnp.zeros((w, w), BF16))

    def mm(x, z):
        return _dot(x.astype(BF16), z)

    def prep(where):
        each = lambda f, *xs: [f(*a) for a in zip(*xs)]
        qc = [qkv_ref[bi, pl.ds(r0, c), 0:w] for bi, r0 in where]
        kc = [qkv_ref[bi, pl.ds(r0, c), w:2 * w] for bi, r0 in where]
        bc = [ab_ref[bi, pl.ds(r0, c), w:2 * w] for bi, r0 in where]
        kb = each(lambda k, b: k * b, kc, bc)
        vb = each(lambda br, b: qkv_ref[br[0], pl.ds(br[1], c), 2 * w:3 * w] * b, where, bc)
        big_g = [ab_ref[bi, pl.ds(r0, c), 0:w] for bi, r0 in where]
        zk = [jnp.where(bd, _dot_tn(k.astype(BF16), tile4), 0.0).astype(BF16) for k in kc]
        stack = lambda x, y: jnp.concatenate([x, y], axis=0)
        kq = each(lambda k, q, z: mm(stack(k, q), z), kb, qc, zk)
        g_row = [jnp.sum(jnp.where(eye_rep, g, 0.0), axis=0, keepdims=True) for g in big_g]
        decay = each(lambda g, r: jnp.exp(jnp.where(incl_rep, g - r, NEG_INF)), big_g, g_row)
        qk = each(lambda x, d: jnp.where(incl_rep, x[c:] * d, 0.0), kq, decay)
        p = each(lambda x, d: jnp.where(strict_rep, -(x[:c] * d), 0.0), kq, decay)
        tinv = [jnp.where(eye_rep, 1.0, 0.0) for _ in p]
        n_sq = int(math.log2(c)) - 1
        for i in range(n_sq):
            both = each(lambda x, tv: mm(stack(x, tv), zmat(x)), p, tinv)
            tinv = each(lambda tv, r: tv + r[c:], tinv, both)
            p = [r[:c] for r in both]
        tinv = each(lambda tv, x: tv + mm(tv, zmat(x)), tinv, p)
        e_g = [jnp.exp(g) for g in big_g]
        wmat = each(lambda tv, k, e: mm(tv, zmat(k * e)), tinv, kb, e_g)
        umat = each(lambda tv, x: mm(tv, zmat(x)), tinv, vb)
        wq = each(lambda wm, q, e: jnp.concatenate([wm, q * e], axis=0).astype(BF16), wmat, qc, e_g)
        k_dec = each(lambda k, g: (k * jnp.exp(g[c - 1:c, :] - g)).astype(BF16), kc, big_g)
        g_last = [jnp.exp(g[c - 1:c, :]) for g in big_g]
        return list(zip(wq, umat, qk, k_dec, g_last))

    def group(gi, carry):
        base = gi * (group_chunks * c)
        r0s = [pl.multiple_of(base + j * c, c) for j in range(group_chunks)]
        pre = prep([(bi, r0) for r0 in r0s for bi in range(nb)])
        states = [state_ref[bi] for bi in range(nb)]
        for j, r0 in enumerate(r0s):
            for bi in range(nb):
                wq, umat, qk, k_dec, gl = pre[j * nb + bi]
                ws_qs = _dot(wq, states[bi].astype(BF16))
                v_new = umat - ws_qs[:c]
                o_s[bi, pl.ds(r0, c), :] = ws_qs[c:] + mm(qk, zmat(v_new))
                upd = _dot_tn(k_dec, v_new.astype(BF16))
                states[bi] = states[bi] * gl + jnp.where(bd, upd, 0.0)
        for bi in range(nb):
            state_ref[bi] = states[bi]
        return carry

    lax.fori_loop(0, tg // (group_chunks * c), group, 0)

    for bi in range(nb):
        o = o_s[bi]
        ms = _dot_exact_rhs(o * o, bd_ones, pieces=2) * (1.0 / DH)
        o_ref[bi] = (o * lax.rsqrt(ms + EPS) * gon_ref[...] * gate_ref[bi]).astype(o_ref.dtype)


def _gdn(gqkv, gab, ggate, lw, *, tg):
    b, t, _ = gqkv.shape
    w = HEADS * DH
    nb = 2 if b % 2 == 0 else 1
    row = lambda c: pl.BlockSpec((nb, tg, c), lambda i, j: (i, j, 0))
    return pl.pallas_call(
        functools.partial(_gdn_kernel, tg=tg, nb=nb, group_chunks=next(
            g for g in (6, 4, 3, 2, 1) if (tg // GDN_CHUNK) % g == 0)),
        grid=(b // nb, t // tg),
        in_specs=[row(3 * w), row(2 * w), row(w), _resident((1, w))],
        out_specs=row(w),
        out_shape=jax.ShapeDtypeStruct((b, t, w), BF16),
        scratch_shapes=[pltpu.VMEM((nb, tg, w), F32), pltpu.VMEM((nb, w, w), F32)],
        compiler_params=_cparams("parallel", "arbitrary"),
        name="gdn",
    )(gqkv, gab, ggate, lw["gdn_gon"])


def _merge_kernel(h_ref, y0_ref, y1_ref, y2_ref, y3_ref, g_ref, wg_ref, bg_ref, wb_ref, wo_ref,
                  o_ref, *, tt):
    t = pl.program_id(1)
    half = tt // 2
    for k in range(2):
        sl = slice(k * half, (k + 1) * half)
        h = h_ref[sl, :]
        rows = t * tt + k * half + _iota((half, 1), 0)
        u = jnp.where(rows >= PAD_LEN, _rms(h, g_ref[...]), 0.0).astype(BF16)
        merged = None
        for n, y_ref in enumerate((y0_ref, y1_ref, y2_ref, y3_ref)):
            gate = _sigmoid(_dot(u, wg_ref[n]) + bg_ref[n])
            term = gate * _dot(y_ref[sl, :], wb_ref[n])
            merged = term if merged is None else merged + term
        o_ref[sl, :] = h + _dot(merged.astype(BF16), wo_ref[...])


def _merge(h, ys, lw, *, tt):
    b, t, d = h.shape
    row = lambda c: pl.BlockSpec((None, tt, c), lambda i, j: (i, j, 0))
    return pl.pallas_call(
        functools.partial(_merge_kernel, tt=tt),
        grid=(b, t // tt),
        in_specs=[row(d)] + [row(D_BRANCH)] * N_BRANCH + [
            _resident((1, d)), _resident(lw["w_gate"].shape), _resident(lw["b_gate"].shape),
            _resident(lw["w_branch"].shape), _resident((d, d))],
        out_specs=row(d),
        out_shape=jax.ShapeDtypeStruct((b, t, d), F32),
        compiler_params=_cparams("parallel", "parallel"),
        name="merge",
    )(h, *ys, lw["ln_mix"], lw["w_gate"], lw["b_gate"], lw["w_branch"], lw["w_out"])


def _rope_swap(wr):
    half = wr.shape[-1] // 2
    return jnp.concatenate([-wr[..., half:], wr[..., :half]], axis=-1)


def _head_blocks(parts, width=128):
    rows = parts[0].shape[0]
    x = jnp.concatenate(parts, axis=-1)
    x = jnp.pad(x, ((0, 0), (0, 0), (0, width - x.shape[-1])))
    return x.reshape(rows, HEADS * width)


def _layer_weights(l, w_in, fox_bf, mla_gq, mla_wq, mla_gkv, mla_wkv, gdn_conv, gdn_alog, gdn_dtb,
                   gdn_gon, lru_conv, lru_conv_b, lru_wa, lru_ba, lru_wx, lru_bx, lru_lam,
                   w_gate, b_gate, w_branch, w_out, ln_mix):
    d = w_in.shape[1]
    wi = w_in[l]
    w = HEADS * DH
    lw = {"ln_mix": ln_mix[l][None]}
    lw["w_fox"] = wi[:, OFF_FOX_QKV:OFF_FOX_F].astype(BF16)
    lw["w_ft"] = jnp.pad(wi[:, OFF_FOX_F:OFF_MLA_CQ].T, ((0, 8 - HEADS), (0, 0))).astype(BF16)
    lw["b_f"] = jnp.pad(fox_bf[l], (0, 8 - HEADS))[:, None]
    z64 = jnp.zeros((d, 64), F32)
    z32 = jnp.zeros((d, 32), F32)
    kr = wi[:, OFF_MLA_KR:OFF_GDN_QKV]
    lw["w_mla"] = jnp.concatenate(
        [wi[:, OFF_MLA_CQ:OFF_MLA_CKV], z64, wi[:, OFF_MLA_CKV:OFF_MLA_KR],
         z64, kr, z32, z64, _rope_swap(kr), z32], axis=-1).astype(BF16)
    lw["g_q"] = jnp.pad(mla_gq[l], (0, 256 - MLA_Q_RANK))[None]
    wq = mla_wq[l].reshape(MLA_Q_RANK, HEADS, MLA_NOPE + MLA_ROPE)
    wq_nope, wq_rope = wq[..., :MLA_NOPE], wq[..., MLA_NOPE:]
    padq = lambda x: jnp.pad(x, ((0, 256 - MLA_Q_RANK), (0, 0))).astype(BF16)
    lw["w_q"] = padq(_head_blocks([wq_nope, wq_rope]))
    lw["w_qs"] = padq(_head_blocks([jnp.zeros_like(wq_nope), _rope_swap(wq_rope)]))
    lw["g_kv"] = mla_gkv[l][None]
    wkv = mla_wkv[l].reshape(MLA_KV_RANK, HEADS, MLA_NOPE + DH)
    lw["w_kv"] = jnp.concatenate(
        [_head_blocks([wkv[..., :MLA_NOPE]]), wkv[..., MLA_NOPE:].reshape(MLA_KV_RANK, w)],
        axis=-1).astype(BF16)
    rep = lambda x: jnp.repeat(x, DH, axis=-1)
    lw["w_gdn"] = jnp.concatenate(
        [wi[:, OFF_GDN_QKV:OFF_GDN_A], rep(wi[:, OFF_GDN_A:OFF_GDN_B]),
         rep(wi[:, OFF_GDN_B:OFF_GDN_G]), wi[:, OFF_GDN_G:OFF_LRU]], axis=-1).astype(BF16)
    lw["gdn_conv"] = gdn_conv[l]
    lw["gdn_alog"] = rep(gdn_alog[l])[None]
    lw["gdn_dtb"] = rep(gdn_dtb[l])[None]
    lw["gdn_gon"] = jnp.tile(gdn_gon[l], HEADS)[None]
    lw["w_lru"] = wi[:, OFF_LRU:N_IN].astype(BF16)
    lw["lru_conv"] = lru_conv[l]
    lw["lru_conv_b"] = lru_conv_b[l][None]
    lw["lru_wa"] = jax.scipy.linalg.block_diag(*lru_wa[l]).astype(BF16)
    lw["lru_ba"] = lru_ba[l][None]
    lw["lru_wx"] = jax.scipy.linalg.block_diag(*lru_wx[l]).astype(BF16)
    lw["lru_bx"] = lru_bx[l][None]
    lw["lru_lam"] = lru_lam[l][None]
    lw["w_gate"] = w_gate[l].astype(BF16)
    lw["b_gate"] = b_gate[l][:, None, :]
    lw["w_branch"] = w_branch[l].astype(BF16)
    lw["w_out"] = w_out[l].astype(BF16)
    return lw


def _rope_tables(t):
    rel = (jnp.arange(t) - PAD_LEN).astype(F32)
    inv_freq = ROPE_BASE ** (-(jnp.arange(0, MLA_ROPE, 2, dtype=F32) / MLA_ROPE))
    ang = rel[:, None] * inv_freq[None, :]
    cos, sin = jnp.cos(ang), jnp.sin(ang)
    pad = jnp.zeros((t, 128 - MLA_NOPE - MLA_ROPE), F32)
    cos128 = jnp.concatenate([jnp.ones((t, MLA_NOPE), F32), cos, cos, pad], axis=-1)
    sin128 = jnp.concatenate([jnp.zeros((t, MLA_NOPE), F32), sin, sin, pad], axis=-1)
    return cos128, sin128


def _ffn_rows(n):
    for cand in (512, 256, 128, 64, 32, 16, 8):
        if n % cand == 0:
            return cand
    raise ValueError(f"row count {n} is not a multiple of 8")


def kernel(x, meta, ln_ffn1, ffn1_wi, ffn1_wo, ln_mix, w_in, fox_bf, mla_gq, mla_wq, mla_gkv, mla_wkv, gdn_conv, gdn_alog, gdn_dtb, gdn_gon, lru_conv, lru_conv_b, lru_wa, lru_ba, lru_wx, lru_bx, lru_lam, w_gate, b_gate, w_branch, w_out, ln_ffn2, ffn2_wi, ffn2_wo, ln_final):
    b, s, d = x.shape
    t = BLOCK + s
    depth = w_in.shape[0]
    h = jnp.concatenate([jnp.zeros((b, PAD_LEN, d), x.dtype),
                         jnp.broadcast_to(meta.astype(x.dtype)[None], (b, N_META, d)), x], axis=1)
    tt = _row_tile(t)
    tm = _ffn_rows(b * t)
    tq = next(c for c in (512, 256, 128) if s % c == 0)
    cos128, sin128 = _rope_tables(t)
    idx = jnp.arange(tt)
    ctri = ((idx[:, None] >= idx[None, :])
            & (idx[:, None] // GDN_CHUNK == idx[None, :] // GDN_CHUNK)).astype(BF16)
    lane_head = jnp.arange(HEADS * DH) // DH
    bd_ones = (lane_head[:, None] == lane_head[None, :]).astype(BF16)
    gf = ln_final[None]
    for l in range(depth):
        lw = _layer_weights(l, w_in, fox_bf, mla_gq, mla_wq, mla_gkv, mla_wkv, gdn_conv, gdn_alog,
                            gdn_dtb, gdn_gon, lru_conv, lru_conv_b, lru_wa, lru_ba, lru_wx, lru_bx,
                            lru_lam, w_gate, b_gate, w_branch, w_out, ln_mix)
        h = _ffn(h.reshape(b * t, d), ln_ffn1[l][None], ffn1_wi[l].astype(BF16),
                 ffn1_wo[l].astype(BF16), gf, tm=tm).reshape(b, t, d)
        (fq, fk, fv, fcum, mq, mk, mv, gqkv, gab, ggate, y_lru) = _inproj(
            h, lw, cos128, sin128, ctri, bd_ones, tt=tt)
        y_fox = _attention(fq, fk, fv, fcum, head_slices=False, tq=tq)
        y_mla = _attention(mq, mk, mv, None, head_slices=True, tq=tq)
        y_gdn = _gdn(gqkv, gab, ggate, lw, tg=tt)
        h = _merge(h, (y_fox, y_mla, y_gdn, y_lru), lw, tt=_merge_tile(t, tt))
        wi2, wo2 = ffn2_wi[l].astype(BF16), ffn2_wo[l].astype(BF16)
        if l == depth - 1:
            return _ffn_out(h, ln_ffn2[l][None], wi2, wo2, gf, tf=tq)
        h = _ffn(h.reshape(b * t, d), ln_ffn2[l][None], wi2, wo2, gf, tm=tm).reshape(b, t, d)
```
